```python
import jax, jax.numpy as jnp
from jax import lax
import numpy as np

D_MODEL = 1024
BATCH = 32
SEQ = 2048
DEPTH = 1
DEC_BATCH = 128
DEC_SEQ = 8
PAST_LEN = 8192
PAGE_SIZE = 128

N_HEADS = 8
HEAD_DIM = 64
ATTN_WIDTH = N_HEADS * HEAD_DIM
CONV_CH = D_MODEL // 2
CONV_WIDTH = 31
D_FF = 4 * D_MODEL
Q_BLOCK = 128
EPS = 1e-6
FORGET_BIAS_INIT = 3.0
POOL_NUM = 5
POOL_DEN = 4
IN_SPLITS = (ATTN_WIDTH, 2 * ATTN_WIDTH, 3 * ATTN_WIDTH, 3 * ATTN_WIDTH + N_HEADS,
             3 * ATTN_WIDTH + N_HEADS + CONV_CH, 3 * ATTN_WIDTH + N_HEADS + 2 * CONV_CH,
             3 * ATTN_WIDTH + N_HEADS + 2 * CONV_CH + D_MODEL)
IN_WIDTH = 3 * ATTN_WIDTH + N_HEADS + 2 * CONV_CH + 2 * D_MODEL

kernel_name = 'fox_conformer_parallel_hybrid_step'


def rms_norm(x, g):
    xf = x.astype(jnp.float32)
    y = xf * lax.rsqrt(jnp.mean(xf * xf, axis=-1, keepdims=True) + EPS)
    return (y * g.astype(jnp.float32)).astype(x.dtype)


def layer_norm(x, g, b):
    xf = x.astype(jnp.float32)
    mu = jnp.mean(xf, axis=-1, keepdims=True)
    var = jnp.mean(jnp.square(xf - mu), axis=-1, keepdims=True)
    y = (xf - mu) * lax.rsqrt(var + EPS)
    return (y * g.astype(jnp.float32) + b.astype(jnp.float32)).astype(x.dtype)


def fox_attention_prompt(q, k, v, logf):
    B, S, H, Dh = q.shape
    scale = Dh ** -0.5
    c = jnp.cumsum(logf, axis=1)
    c_key = jnp.transpose(c, (0, 2, 1))
    nb = S // Q_BLOCK
    qb = jnp.transpose(q.reshape(B, nb, Q_BLOCK, H, Dh), (1, 0, 2, 3, 4))
    cb = jnp.transpose(c.reshape(B, nb, Q_BLOCK, H), (1, 0, 3, 2))
    key_pos = jnp.arange(S)

    def block(args):
        i, q_i, c_i = args
        s = jnp.einsum('bqhd,bkhd->bhqk', q_i, k).astype(jnp.float32) * scale
        bias = c_i[..., :, None] - c_key[..., None, :]
        q_pos = i * Q_BLOCK + jnp.arange(Q_BLOCK)
        mask = key_pos[None, :] <= q_pos[:, None]
        s = jnp.where(mask, s + bias, -jnp.inf)
        p = jax.nn.softmax(s, axis=-1)
        return jnp.einsum('bhqk,bkhd->bqhd', p.astype(v.dtype), v)

    o = lax.map(block, (jnp.arange(nb), qb, cb))
    return jnp.transpose(o, (1, 0, 2, 3, 4)).reshape(B, S, H * Dh)


def fox_attention_sample(q, k_new, v_new, logf_new, cache_k, cache_v, cache_logf, page_table):
    DB, T, H, Dh = q.shape
    scale = Dh ** -0.5
    past = page_table.shape[1] * cache_k.shape[1]
    k_past = cache_k[page_table].reshape(DB, past, H, Dh)
    v_past = cache_v[page_table].reshape(DB, past, H, Dh)
    lf_past = cache_logf[page_table].reshape(DB, past, H).astype(jnp.float32)
    cum_past = jnp.cumsum(lf_past, axis=1)
    c_past = jnp.transpose(cum_past - cum_past[:, -1:], (0, 2, 1))
    c_new = jnp.transpose(jnp.cumsum(logf_new, axis=1), (0, 2, 1))
    s_past = jnp.einsum('bqhd,bkhd->bhqk', q, k_past).astype(jnp.float32) * scale \
        + (c_new[..., :, None] - c_past[..., None, :])
    s_new = jnp.einsum('bqhd,bkhd->bhqk', q, k_new).astype(jnp.float32) * scale \
        + (c_new[..., :, None] - c_new[..., None, :])
    causal = jnp.arange(T)[None, :] <= jnp.arange(T)[:, None]
    s_new = jnp.where(causal, s_new, -jnp.inf)
    p = jax.nn.softmax(jnp.concatenate([s_past, s_new], axis=-1), axis=-1)
    o = jnp.einsum('bhqk,bkhd->bqhd', p[..., :past].astype(v_past.dtype), v_past) \
        + jnp.einsum('bhqk,bkhd->bqhd', p[..., past:].astype(v_new.dtype), v_new)
    return o.reshape(DB, T, H * Dh)


def depthwise_conv(u_ext, w_dw, b_dw):
    y = lax.conv_general_dilated(u_ext, w_dw[:, None, :].astype(u_ext.dtype), window_strides=(1,),
                                 padding='VALID', dimension_numbers=('NWC', 'WIO', 'NWC'),
                                 feature_group_count=u_ext.shape[-1])
    return y + b_dw


def run_layer(x, attend, extend, norm1_g, w_in, b_forget, b_gate, q_norm_g, k_norm_g, w_attn_out,
              conv_dw_w, conv_dw_b, conv_ln_g, conv_ln_b, w_conv_out, w_out, norm2_g, w_up, w_down):
    lead = x.shape[:-1]
    h = rms_norm(x, norm1_g)
    z = h @ w_in
    zq, zk, zv, zf, za, zu, zga, zgb = jnp.split(z, IN_SPLITS, axis=-1)
    q = rms_norm(zq.reshape(*lead, N_HEADS, HEAD_DIM), q_norm_g)
    k = rms_norm(zk.reshape(*lead, N_HEADS, HEAD_DIM), k_norm_g)
    v = zv.reshape(*lead, N_HEADS, HEAD_DIM)
    logf = jax.nn.log_sigmoid((zf + b_forget).astype(jnp.float32))
    y_attn = attend(q, k, v, logf) @ w_attn_out
    u = za * jax.nn.sigmoid(zu)
    u_ext, conv_state = extend(u)
    c = depthwise_conv(u_ext, conv_dw_w, conv_dw_b)
    y_conv = jax.nn.silu(layer_norm(c, conv_ln_g, conv_ln_b)) @ w_conv_out
    gate_a = jax.nn.sigmoid(zga + b_gate[:D_MODEL])
    gate_b = jax.nn.sigmoid(zgb + b_gate[D_MODEL:])
    x = x + (gate_a * y_attn + gate_b * y_conv) @ w_out
    h2 = rms_norm(x, norm2_g)
    x = x + jnp.square(jax.nn.relu(h2 @ w_up)) @ w_down
    return x, k, v, logf, conv_state


def setup_inputs(seed: int = 0) -> dict:
    key = jax.random.key(seed)
    ks = jax.random.split(key, 24)
    f32 = jnp.float32
    n_pages = PAST_LEN // PAGE_SIZE
    n_pool = (DEC_BATCH * n_pages * POOL_NUM) // POOL_DEN

    def nrm(k, shape, scale):
        return jax.random.normal(k, shape, f32) * scale

    page_table = jax.random.permutation(ks[6], n_pool)[:DEC_BATCH * n_pages]
    page_table = page_table.reshape(DEC_BATCH, n_pages).astype(jnp.int32)
    return {
        'x_prompt': nrm(ks[0], (BATCH, SEQ, D_MODEL), 1.0),
        'x_sample': nrm(ks[1], (DEC_BATCH, DEC_SEQ, D_MODEL), 1.0),
        'cache_k': nrm(ks[2], (DEPTH, n_pool, PAGE_SIZE, N_HEADS, HEAD_DIM), 1.0),
        'cache_v': nrm(ks[3], (DEPTH, n_pool, PAGE_SIZE, N_HEADS, HEAD_DIM), 1.0),
        'cache_logf': jax.nn.log_sigmoid(FORGET_BIAS_INIT + nrm(ks[4], (DEPTH, n_pool, PAGE_SIZE, N_HEADS), 1.0)),
        'state_conv': nrm(ks[5], (DEPTH, DEC_BATCH, CONV_WIDTH - 1, CONV_CH), 1.0),
        'page_table': page_table,
        'norm1_g': 1.0 + nrm(ks[7], (DEPTH, D_MODEL), 0.05),
        'w_in': nrm(ks[8], (DEPTH, D_MODEL, IN_WIDTH), D_MODEL ** -0.5),
        'b_forget': FORGET_BIAS_INIT + nrm(ks[9], (DEPTH, N_HEADS), 0.5),
        'b_gate': nrm(ks[10], (DEPTH, 2 * D_MODEL), 0.1),
        'q_norm_g': 1.0 + nrm(ks[11], (DEPTH, HEAD_DIM), 0.05),
        'k_norm_g': 1.0 + nrm(ks[12], (DEPTH, HEAD_DIM), 0.05),
        'w_attn_out': nrm(ks[13], (DEPTH, ATTN_WIDTH, D_MODEL), ATTN_WIDTH ** -0.5),
        'conv_dw_w': nrm(ks[14], (DEPTH, CONV_WIDTH, CONV_CH), CONV_WIDTH ** -0.5),
        'conv_dw_b': nrm(ks[15], (DEPTH, CONV_CH), 0.02),
        'conv_ln_g': 1.0 + nrm(ks[16], (DEPTH, CONV_CH), 0.05),
        'conv_ln_b': nrm(ks[17], (DEPTH, CONV_CH), 0.02),
        'w_conv_out': nrm(ks[18], (DEPTH, CONV_CH, D_MODEL), CONV_CH ** -0.5),
        'w_out': nrm(ks[19], (DEPTH, D_MODEL, D_MODEL), D_MODEL ** -0.5),
        'norm2_g': 1.0 + nrm(ks[20], (DEPTH, D_MODEL), 0.05),
        'w_up': nrm(ks[21], (DEPTH, D_MODEL, D_FF), D_MODEL ** -0.5),
        'w_down': nrm(ks[22], (DEPTH, D_FF, D_MODEL), D_FF ** -0.5),
    }


def reference(x_prompt, x_sample, cache_k, cache_v, cache_logf, state_conv, page_table,
              norm1_g, w_in, b_forget, b_gate, q_norm_g, k_norm_g, w_attn_out,
              conv_dw_w, conv_dw_b, conv_ln_g, conv_ln_b, w_conv_out, w_out, norm2_g, w_up, w_down):
    xp, xs = x_prompt, x_sample
    kp_l, vp_l, lp_l, cp_l = [], [], [], []
    ks_l, vs_l, ls_l, cs_l = [], [], [], []
    for l in range(DEPTH):
        lw = (norm1_g[l], w_in[l], b_forget[l], b_gate[l], q_norm_g[l], k_norm_g[l], w_attn_out[l],
              conv_dw_w[l], conv_dw_b[l], conv_ln_g[l], conv_ln_b[l], w_conv_out[l], w_out[l],
              norm2_g[l], w_up[l], w_down[l])
        extend_p = lambda u: (jnp.pad(u, ((0, 0), (CONV_WIDTH - 1, 0), (0, 0))), u[:, -(CONV_WIDTH - 1):])
        xp, kp, vp, lp, cp = run_layer(xp, fox_attention_prompt, extend_p, *lw)
        attend_s = lambda q, k, v, lf, l=l: fox_attention_sample(q, k, v, lf, cache_k[l], cache_v[l],
                                                                  cache_logf[l], page_table)

        def extend_s(u, l=l):
            ext = jnp.concatenate([state_conv[l].astype(u.dtype), u], axis=1)
            return ext, ext[:, -(CONV_WIDTH - 1):]

        xs, k_s, v_s, l_s, c_s = run_layer(xs, attend_s, extend_s, *lw)
        kp_l.append(kp); vp_l.append(vp); lp_l.append(lp); cp_l.append(cp)
        ks_l.append(k_s); vs_l.append(v_s); ls_l.append(l_s); cs_l.append(c_s)
    k_prompt = jnp.stack(kp_l)
    v_prompt = jnp.stack(vp_l)
    logf_prompt = jnp.stack(lp_l)
    conv_prompt = jnp.stack(cp_l)
    k_sample = jnp.stack(ks_l)
    v_sample = jnp.stack(vs_l)
    logf_sample = jnp.stack(ls_l)
    conv_sample = jnp.stack(cs_l)
    return (xp, xs, k_prompt, v_prompt, logf_prompt, conv_prompt, k_sample, v_sample, logf_sample, conv_sample)
```

```python
import functools

import jax
import jax.numpy as jnp
from jax import lax
from jax.experimental import pallas as pl
from jax.experimental.pallas import tpu as pltpu

N_HEADS = 8
HEAD_DIM = 64
ATTN_WIDTH = N_HEADS * HEAD_DIM
CONV_WIDTH = 31
EPS = 1e-6
NEG_BIG = -1e30

LANES = 128
SUBLANES = 8
HEADS_PER_VREG = LANES // HEAD_DIM
VMEM_LIMIT_BYTES = 56 * 1024 * 1024

F32 = jnp.float32
BF16 = jnp.bfloat16

NT_DIMS = (((1,), (1,)), ((), ()))


def _const_spec(shape):
    nd = len(shape)
    return pl.BlockSpec(shape, lambda *_: (0,) * nd, pipeline_mode=pl.Buffered(1))


def _split3(x):
    hi = x.astype(BF16)
    r = x - hi.astype(F32)
    mid = r.astype(BF16)
    lo = (r - mid.astype(F32)).astype(BF16)
    return hi, mid, lo


def _log_sigmoid(x):
    return jnp.minimum(x, 0.0) - jnp.log1p(jnp.exp(-jnp.abs(x)))


def _dot(a, b):
    return jnp.dot(a, b, preferred_element_type=F32)


def _in_proj_body(x_ref, g1_ref, w_ref, wf_ref, bf_ref, bg_ref, qg_ref, kg_ref, hm_ref, *rest,
                  tiles_per_seq, with_cumsum):
    if with_cumsum:
        (wft_ref, bft_ref, tri_ref,
         q_ref, k_ref, v_ref, lf_ref, u_ref, ga_ref, gb_ref, ct_ref, carry_ref) = rest
    else:
        q_ref, k_ref, v_ref, lf_ref, u_ref, ga_ref, gb_ref = rest

    x = x_ref[...]
    ms = jnp.mean(x * x, axis=-1, keepdims=True)
    h = (x * lax.rsqrt(ms + EPS) * g1_ref[...]).astype(BF16)

    A = ATTN_WIDTH
    d_model = x.shape[1]
    hm = hm_ref[...]

    def head_rms(z, g):
        msq = _dot((z * z).astype(BF16), hm)
        return z * lax.rsqrt(msq + EPS) * g

    zq = _dot(h, w_ref[:, 0:A])
    q_ref[...] = head_rms(zq, qg_ref[...]).astype(q_ref.dtype)
    zk = _dot(h, w_ref[:, A:2 * A])
    k_ref[...] = head_rms(zk, kg_ref[...])
    v_ref[...] = _dot(h, w_ref[:, 2 * A:3 * A])

    zf = _dot(h, wf_ref[...])
    lf_ref[...] = _log_sigmoid(zf + bf_ref[...])

    za = _dot(h, w_ref[:, 3 * A:4 * A])
    zu = _dot(h, w_ref[:, 4 * A:5 * A])
    u_ref[...] = za * jax.nn.sigmoid(zu)

    o = 5 * A
    zga = _dot(h, w_ref[:, o:o + d_model])
    ga_ref[...] = jax.nn.sigmoid(zga + bg_ref[:, 0:d_model]).astype(ga_ref.dtype)
    zgb = _dot(h, w_ref[:, o + d_model:o + 2 * d_model])
    gb_ref[...] = jax.nn.sigmoid(zgb + bg_ref[:, d_model:2 * d_model]).astype(gb_ref.dtype)

    if with_cumsum:
        @pl.when(pl.program_id(0) % tiles_per_seq == 0)
        def _():
            carry_ref[...] = jnp.zeros_like(carry_ref)

        zft = lax.dot_general(wft_ref[...], h, NT_DIMS, preferred_element_type=F32)
        lft = _log_sigmoid(zft + bft_ref[...])
        tri = tri_ref[...]
        c = carry_ref[:, 0:1]
        for part in _split3(lft):
            c = c + _dot(part, tri)
        ct_ref[...] = c
        carry_ref[...] = jnp.broadcast_to(c[:, -1:], carry_ref.shape)


def _in_proj(x2d, p, *, tm, seq_len, with_cumsum, q_dtype):
    n, d_model = x2d.shape
    assert n % tm == 0
    n_tiles = n // tm
    A = ATTN_WIDTH
    wcols = p["w_cat"].shape[1]

    row = lambda c: pl.BlockSpec((tm, c), lambda i: (i, 0))
    in_specs = [row(d_model), _const_spec((1, d_model)), _const_spec((d_model, wcols)),
                _const_spec((d_model, N_HEADS)), _const_spec((1, N_HEADS)),
                _const_spec((1, 2 * d_model)), _const_spec((1, A)), _const_spec((1, A)),
                _const_spec((A, A))]
    args = [x2d, p["g1"], p["w_cat"], p["w_f"], p["b_f"], p["b_gate"], p["qg"], p["kg"], p["head_mean"]]
    out_shape = [jax.ShapeDtypeStruct((n, A), q_dtype), jax.ShapeDtypeStruct((n, A), F32),
                 jax.ShapeDtypeStruct((n, A), F32), jax.ShapeDtypeStruct((n, N_HEADS), F32),
                 jax.ShapeDtypeStruct((n, A), F32), jax.ShapeDtypeStruct((n, d_model), BF16),
                 jax.ShapeDtypeStruct((n, d_model), BF16)]
    out_specs = [row(A), row(A), row(A), row(N_HEADS), row(A), row(d_model), row(d_model)]
    scratch = []
    tiles_per_seq = 1
    if with_cumsum:
        assert seq_len % tm == 0
        tiles_per_seq = seq_len // tm
        tri = (lax.broadcasted_iota(jnp.int32, (tm, tm), 0)
               <= lax.broadcasted_iota(jnp.int32, (tm, tm), 1)).astype(BF16)
        in_specs += [_const_spec((N_HEADS, d_model)), _const_spec((N_HEADS, 1)), _const_spec((tm, tm))]
        args += [p["w_f_t"], p["b_f_t"], tri]
        out_shape.append(jax.ShapeDtypeStruct((N_HEADS, n), F32))
        out_specs.append(pl.BlockSpec((N_HEADS, tm), lambda i: (0, i)))
        scratch.append(pltpu.VMEM((N_HEADS, LANES), F32))

    return pl.pallas_call(
        functools.partial(_in_proj_body, tiles_per_seq=tiles_per_seq, with_cumsum=with_cumsum),
        grid=(n_tiles,),
        in_specs=in_specs,
        out_specs=out_specs,
        out_shape=out_shape,
        scratch_shapes=scratch,
        compiler_params=pltpu.CompilerParams(dimension_semantics=("arbitrary",),
                                             vmem_limit_bytes=VMEM_LIMIT_BYTES),
        name="in_proj_cumsum" if with_cumsum else "in_proj",
    )(*args)


def _attn_prompt_body(q_ref, k_ref, v_ref, c_ref, o_ref, kb_ref, vb_ref, *, tq, tk):
    hp = pl.program_id(1)
    seq = q_ref.shape[0]
    kb_ref[...] = k_ref[...].astype(BF16)
    vb_ref[...] = v_ref[...].astype(BF16)
    lane = lax.broadcasted_iota(jnp.int32, (1, LANES), 1)
    first_head = lane < HEAD_DIM
    causal = (lax.broadcasted_iota(jnp.int32, (tq, tk), 1)
              <= lax.broadcasted_iota(jnp.int32, (tq, tk), 0))

    def q_block(i, _):
        q0 = pl.multiple_of(i * tq, tq)
        q = q_ref[pl.ds(q0, tq), :]
        outs = []
        for hh in range(HEADS_PER_VREG):
            head = HEADS_PER_VREG * hp + hh
            own = first_head if hh == 0 else jnp.logical_not(first_head)
            qm = jnp.where(own, q, jnp.zeros_like(q))
            c0 = c_ref[pl.ds(head, 1), pl.ds(q0, tq)][:, 0:1]

            def kv_step(j, carry, masked):
                m, l, acc = carry
                k0 = pl.multiple_of(j * tk, tk)
                s = lax.dot_general(qm, kb_ref[pl.ds(k0, tk), :], NT_DIMS, preferred_element_type=F32)
                s = s + (c0 - c_ref[pl.ds(head, 1), pl.ds(k0, tk)])
                if masked:
                    s = jnp.where(causal, s, NEG_BIG)
                m_new = jnp.maximum(m, jnp.max(s, axis=1, keepdims=True))
                alpha = jnp.exp(m - m_new)
                pr = jnp.exp(s - m_new)
                l = alpha * l + jnp.sum(pr, axis=1, keepdims=True)
                acc = alpha * acc + _dot(pr.astype(BF16), vb_ref[pl.ds(k0, tk), :])
                return m_new, l, acc

            init = (jnp.full((tq, 1), NEG_BIG, F32), jnp.zeros((tq, 1), F32), jnp.zeros((tq, LANES), F32))
            carry = lax.fori_loop(0, i, functools.partial(kv_step, masked=False), init)
            _, l, acc = kv_step(i, carry, masked=True)
            outs.append(acc * (1.0 / l))
        o_ref[pl.ds(q0, tq), :] = jnp.where(first_head, outs[0], outs[1]).astype(o_ref.dtype)
        return 0

    lax.fori_loop(0, seq // tq, q_block, 0)


def _attn_prompt(q2d, k2d, v2d, c_t, *, batch, seq_len, tq):
    n = q2d.shape[0]
    pairs = N_HEADS // HEADS_PER_VREG
    blk = pl.BlockSpec((seq_len, LANES), lambda b, hp: (b, hp))
    return pl.pallas_call(
        functools.partial(_attn_prompt_body, tq=tq, tk=tq),
        grid=(batch, pairs),
        in_specs=[blk, blk, blk, pl.BlockSpec((N_HEADS, seq_len), lambda b, hp: (0, b))],
        out_specs=blk,
        out_shape=jax.ShapeDtypeStruct((n, ATTN_WIDTH), BF16),
        scratch_shapes=[pltpu.VMEM((seq_len, LANES), BF16), pltpu.VMEM((seq_len, LANES), BF16)],
        compiler_params=pltpu.CompilerParams(dimension_semantics=("arbitrary", "arbitrary"),
                                             vmem_limit_bytes=VMEM_LIMIT_BYTES),
        name="attn_prompt",
    )(q2d, k2d, v2d, c_t)


def _attn_sample_body(pt_ref, q_ref, ks_ref, vs_ref, lfs_ref, *rest, n_pg, dec_seq):
    kp = rest[0:n_pg]
    vp = rest[n_pg:2 * n_pg]
    lp = rest[2 * n_pg:3 * n_pg]
    lincl_ref, sel_ref = rest[3 * n_pg:3 * n_pg + 2]
    o_ref = rest[3 * n_pg + 2]
    qbd_ref, m_ref, l_ref, acc_ref, tot_ref, lfpad_ref, new_ref = rest[3 * n_pg + 3:]
    del pt_ref

    g = pl.program_id(1)
    rows = N_HEADS * dec_seq
    page = lfpad_ref.shape[0]
    row_head = lax.broadcasted_iota(jnp.int32, (rows, ATTN_WIDTH), 0) // dec_seq
    col_head = lax.broadcasted_iota(jnp.int32, (rows, ATTN_WIDTH), 1) // HEAD_DIM
    own_head = row_head == col_head

    def process(k_blk, v_blk, lf_blk, mask):
        lfpad_ref[:, 0:N_HEADS] = lf_blk
        lincl = lincl_ref[...]
        pre = jnp.zeros((page, LANES), F32)
        for part in _split3(lfpad_ref[...]):
            pre = pre + _dot(lincl, part)
        sel = sel_ref[...]
        pre_rows = jnp.zeros((rows, page), F32)
        for part in _split3(pre):
            pre_rows = pre_rows + lax.dot_general(sel, part, NT_DIMS, preferred_element_type=F32)
        blk_tot = pre_rows[:, page - 1:page]
        tot_new = tot_ref[...] + blk_tot
        s = lax.dot_general(qbd_ref[...], k_blk.astype(BF16), NT_DIMS, preferred_element_type=F32)
        s = s + (tot_new - pre_rows)
        if mask is not None:
            s = jnp.where(mask, s, NEG_BIG)
        m = m_ref[...]
        m_new = jnp.maximum(m, jnp.max(s, axis=1, keepdims=True))
        alpha = jnp.exp(m - m_new)
        pr = jnp.exp(s - m_new)
        l_ref[...] = alpha * l_ref[...] + jnp.sum(pr, axis=1, keepdims=True)
        acc_ref[...] = alpha * acc_ref[...] + _dot(pr.astype(BF16), v_blk.astype(BF16))
        m_ref[...] = m_new
        tot_ref[...] = tot_new

    @pl.when(g == 0)
    def _():
        q = q_ref[...]
        q_rows = jnp.concatenate([q] * N_HEADS, axis=0)
        qbd_ref[...] = jnp.where(own_head, q_rows, 0.0).astype(BF16)
        m_ref[...] = jnp.full(m_ref.shape, NEG_BIG, F32)
        l_ref[...] = jnp.zeros(l_ref.shape, F32)
        acc_ref[...] = jnp.zeros(acc_ref.shape, F32)
        tot_ref[...] = jnp.zeros(tot_ref.shape, F32)
        lfpad_ref[...] = jnp.zeros(lfpad_ref.shape, F32)
        key = lax.broadcasted_iota(jnp.int32, (rows, page), 1)
        t = lax.broadcasted_iota(jnp.int32, (rows, page), 0) % dec_seq
        new_ref[0] = jnp.zeros(new_ref.shape[1:], F32)
        new_ref[0, 0:dec_seq, :] = ks_ref[...]
        k_new = new_ref[0]
        new_ref[1] = jnp.zeros(new_ref.shape[1:], F32)
        new_ref[1, 0:dec_seq, :] = vs_ref[...]
        v_new = new_ref[1]
        new_ref[2] = jnp.zeros(new_ref.shape[1:], F32)
        new_ref[2, 0:dec_seq, 0:N_HEADS] = lfs_ref[...]
        lf_new = new_ref[2, :, 0:N_HEADS]
        process(k_new, v_new, lf_new, key <= t)

    for i in reversed(range(n_pg)):
        process(kp[i][...], vp[i][...], lp[i][...], None)

    @pl.when(g == pl.num_programs(1) - 1)
    def _():
        o = jnp.where(own_head, acc_ref[...] * (1.0 / l_ref[...]), 0.0)
        out = o[0:dec_seq, :]
        for hd in range(1, N_HEADS):
            out = out + o[hd * dec_seq:(hd + 1) * dec_seq, :]
        o_ref[...] = out


def _attn_sample(q3, k3, v3, lf3, cache_k, cache_v, cache_lf, page_table, *, n_pg):
    dec_batch, dec_seq, _ = q3.shape
    n_pool, page, _ = cache_k.shape
    n_pages = page_table.shape[1]
    assert n_pages % n_pg == 0 and dec_seq == SUBLANES and page == LANES
    steps = n_pages // n_pg
    rows = N_HEADS * dec_seq

    lincl = (lax.broadcasted_iota(jnp.int32, (page, page), 1)
             <= lax.broadcasted_iota(jnp.int32, (page, page), 0)).astype(BF16)
    sel = (lax.broadcasted_iota(jnp.int32, (rows, LANES), 0) // dec_seq
           == lax.broadcasted_iota(jnp.int32, (rows, LANES), 1)).astype(BF16)

    def page_map(slot):
        def index_map(b, g, pt):
            return (pt[b * n_pages + n_pages - (g + 1) * n_pg + slot], 0, 0)
        return index_map

    per_b = lambda c: pl.BlockSpec((None, dec_seq, c), lambda b, g, pt: (b, 0, 0))
    in_specs = [per_b(ATTN_WIDTH), per_b(ATTN_WIDTH), per_b(ATTN_WIDTH), per_b(N_HEADS)]
    in_specs += [pl.BlockSpec((None, page, ATTN_WIDTH), page_map(i)) for i in range(n_pg)]
    in_specs += [pl.BlockSpec((None, page, ATTN_WIDTH), page_map(i)) for i in range(n_pg)]
    in_specs += [pl.BlockSpec((None, page, N_HEADS), page_map(i)) for i in range(n_pg)]
    in_specs += [pl.BlockSpec((page, page), lambda b, g, pt: (0, 0)),
                 pl.BlockSpec((rows, LANES), lambda b, g, pt: (0, 0))]
    grid_spec = pltpu.PrefetchScalarGridSpec(
        num_scalar_prefetch=1,
        grid=(dec_batch, steps),
        in_specs=in_specs,
        out_specs=pl.BlockSpec((None, dec_seq, ATTN_WIDTH), lambda b, g, pt: (b, 0, 0)),
        scratch_shapes=[pltpu.VMEM((rows, ATTN_WIDTH), BF16), pltpu.VMEM((rows, 1), F32),
                        pltpu.VMEM((rows, 1), F32), pltpu.VMEM((rows, ATTN_WIDTH), F32),
                        pltpu.VMEM((rows, 1), F32), pltpu.VMEM((page, LANES), F32),
                        pltpu.VMEM((3, page, ATTN_WIDTH), F32)],
    )
    args = [page_table.reshape(-1), q3, k3, v3, lf3]
    args += [cache_k] * n_pg + [cache_v] * n_pg + [cache_lf] * n_pg + [lincl, sel]
    return pl.pallas_call(
        functools.partial(_attn_sample_body, n_pg=n_pg, dec_seq=dec_seq),
        grid_spec=grid_spec,
        out_shape=jax.ShapeDtypeStruct((dec_batch, dec_seq, ATTN_WIDTH), F32),
        compiler_params=pltpu.CompilerParams(dimension_semantics=("arbitrary", "arbitrary"),
                                             vmem_limit_bytes=VMEM_LIMIT_BYTES),
        name="attn_sample",
    )(*args)


def _ln_swish(c, g, b):
    mu = jnp.mean(c, axis=-1, keepdims=True)
    d = c - mu
    var = jnp.mean(d * d, axis=-1, keepdims=True)
    y = d * lax.rsqrt(var + EPS) * g + b
    return y * jax.nn.sigmoid(y)


HIST_ROWS = 32


def _conv_prompt_body(u_ref, w_ref, b_ref, g_ref, beta_ref, o_ref, ext_ref, *, rows_per_chunk):
    tc = u_ref.shape[0]
    pad = HIST_ROWS - (CONV_WIDTH - 1)

    @pl.when(pl.program_id(1) == 0)
    def _():
        ext_ref[0:HIST_ROWS, :] = jnp.zeros((HIST_ROWS, ext_ref.shape[1]), F32)

    ext_ref[HIST_ROWS:HIST_ROWS + tc, :] = u_ref[...]

    for r0 in range(0, tc, rows_per_chunk):
        acc = jnp.zeros((rows_per_chunk, ext_ref.shape[1]), F32) + b_ref[...]
        for k in range(CONV_WIDTH):
            acc = acc + ext_ref[r0 + pad + k:r0 + pad + k + rows_per_chunk, :] * w_ref[k:k + 1, :]
        o_ref[r0:r0 + rows_per_chunk, :] = _ln_swish(acc, g_ref[...], beta_ref[...]).astype(o_ref.dtype)
    ext_ref[0:HIST_ROWS, :] = ext_ref[tc:tc + HIST_ROWS, :]


def _conv_prompt(u2d, p, *, batch, seq_len, tc):
    n, ch = u2d.shape
    tiles = seq_len // tc
    return pl.pallas_call(
        functools.partial(_conv_prompt_body, rows_per_chunk=64),
        grid=(batch, tiles),
        in_specs=[pl.BlockSpec((tc, ch), lambda b, i: (b * tiles + i, 0)),
                  _const_spec((CONV_WIDTH, ch)), _const_spec((1, ch)), _const_spec((1, ch)),
                  _const_spec((1, ch))],
        out_specs=pl.BlockSpec((tc, ch), lambda b, i: (b * tiles + i, 0)),
        out_shape=jax.ShapeDtypeStruct((n, ch), BF16),
        scratch_shapes=[pltpu.VMEM((HIST_ROWS + tc, ch), F32)],
        compiler_params=pltpu.CompilerParams(dimension_semantics=("arbitrary", "arbitrary"),
                                             vmem_limit_bytes=VMEM_LIMIT_BYTES),
        name="conv_prompt",
    )(u2d, p["conv_w"], p["conv_b"], p["ln_g"], p["ln_b"])


def _conv_sample_body(st_ref, u_ref, w_ref, b_ref, g_ref, beta_ref, o_ref, ns_ref, ext_ref):
    hist = CONV_WIDTH - 1
    dec_seq = u_ref.shape[1]

    def one_seq(sb, _):
        ext_ref[0:hist, :] = st_ref[sb]
        ext_ref[hist:hist + dec_seq, :] = u_ref[sb]
        acc = jnp.zeros((dec_seq, ext_ref.shape[1]), F32) + b_ref[...]
        for k in range(CONV_WIDTH):
            acc = acc + ext_ref[k:k + dec_seq, :] * w_ref[k:k + 1, :]
        o_ref[sb] = _ln_swish(acc, g_ref[...], beta_ref[...]).astype(o_ref.dtype)
        ns_ref[sb] = ext_ref[dec_seq:dec_seq + hist, :]
        return 0

    lax.fori_loop(0, u_ref.shape[0], one_seq, 0)


def _conv_sample(state, u3, p, *, seqs_per_step):
    dec_batch, dec_seq, ch = u3.shape
    hist = CONV_WIDTH - 1
    blk = lambda r: pl.BlockSpec((seqs_per_step, r, ch), lambda i: (i, 0, 0))
    return pl.pallas_call(
        _conv_sample_body,
        grid=(dec_batch // seqs_per_step,),
        in_specs=[blk(hist), blk(dec_seq), _const_spec((CONV_WIDTH, ch)), _const_spec((1, ch)),
                  _const_spec((1, ch)), _const_spec((1, ch))],
        out_specs=[blk(dec_seq), blk(hist)],
        out_shape=[jax.ShapeDtypeStruct((dec_batch, dec_seq, ch), F32),
                   jax.ShapeDtypeStruct((dec_batch, hist, ch), F32)],
        scratch_shapes=[pltpu.VMEM((hist + dec_seq + 2, ch), F32)],
        compiler_params=pltpu.CompilerParams(dimension_semantics=("arbitrary",),
                                             vmem_limit_bytes=VMEM_LIMIT_BYTES),
        name="conv_sample",
    )(state, u3, p["conv_w"], p["conv_b"], p["ln_g"], p["ln_b"])


def _merge_mlp_body(x_ref, o_ref, ac_ref, ga_ref, gb_ref, wao_ref, wco_ref, wout_ref, g2_ref,
                    wup_ref, wdn_ref, y_ref, *, ff_chunk):
    ya = _dot(o_ref[...].astype(BF16), wao_ref[...])
    yc = _dot(ac_ref[...].astype(BF16), wco_ref[...])
    mix = ga_ref[...].astype(F32) * ya + gb_ref[...].astype(F32) * yc
    x1 = x_ref[...] + _dot(mix.astype(BF16), wout_ref[...])
    ms = jnp.mean(x1 * x1, axis=-1, keepdims=True)
    h2 = (x1 * lax.rsqrt(ms + EPS) * g2_ref[...]).astype(BF16)
    d_ff = wup_ref.shape[1]
    acc = x1
    for c in range(d_ff // ff_chunk):
        a = jnp.maximum(_dot(h2, wup_ref[:, c * ff_chunk:(c + 1) * ff_chunk]), 0.0)
        acc = acc + _dot((a * a).astype(BF16), wdn_ref[c * ff_chunk:(c + 1) * ff_chunk, :])
    y_ref[...] = acc


def _merge_mlp(x2d, o2d, ac2d, ga, gb, p, *, tm):
    n, d_model = x2d.shape
    d_ff = p["w_up"].shape[1]
    ch = ac2d.shape[1]
    row = lambda c: pl.BlockSpec((tm, c), lambda i: (i, 0))
    return pl.pallas_call(
        functools.partial(_merge_mlp_body, ff_chunk=512),
        grid=(n // tm,),
        in_specs=[row(d_model), row(ATTN_WIDTH), row(ch), row(d_model), row(d_model),
                  _const_spec((ATTN_WIDTH, d_model)), _const_spec((ch, d_model)),
                  _const_spec((d_model, d_model)), _const_spec((1, d_model)),
                  _const_spec((d_model, d_ff)), _const_spec((d_ff, d_model))],
        out_specs=row(d_model),
        out_shape=jax.ShapeDtypeStruct((n, d_model), F32),
        compiler_params=pltpu.CompilerParams(dimension_semantics=("arbitrary",),
                                             vmem_limit_bytes=VMEM_LIMIT_BYTES),
        name="merge_mlp",
    )(x2d, o2d, ac2d, ga, gb, p["w_attn_out"], p["w_conv_out"], p["w_out"], p["g2"], p["w_up"], p["w_down"])


def _prep_layer_params(norm1_g, w_in, b_forget, b_gate, q_norm_g, k_norm_g, w_attn_out, conv_dw_w,
                       conv_dw_b, conv_ln_g, conv_ln_b, w_conv_out, w_out, norm2_g, w_up, w_down):
    A = ATTN_WIDTH
    d_model = w_in.shape[0]
    ch = conv_dw_w.shape[1]
    f0 = 3 * A
    a0 = f0 + N_HEADS
    w_f = w_in[:, f0:a0].astype(BF16)
    w_cat = jnp.concatenate([w_in[:, 0:f0], w_in[:, a0:]], axis=1).astype(BF16)
    hm = (lax.broadcasted_iota(jnp.int32, (A, A), 0) // HEAD_DIM
          == lax.broadcasted_iota(jnp.int32, (A, A), 1) // HEAD_DIM).astype(BF16) * (1.0 / HEAD_DIM)
    return dict(
        g1=norm1_g.reshape(1, d_model), w_cat=w_cat, w_f=w_f, w_f_t=w_f.T,
        b_f=b_forget.reshape(1, N_HEADS), b_f_t=b_forget.reshape(N_HEADS, 1),
        b_gate=b_gate.reshape(1, 2 * d_model),
        qg=jnp.tile(q_norm_g, N_HEADS).reshape(1, A) * (HEAD_DIM ** -0.5),
        kg=jnp.tile(k_norm_g, N_HEADS).reshape(1, A),
        head_mean=hm.astype(BF16),
        conv_w=conv_dw_w, conv_b=conv_dw_b.reshape(1, ch), ln_g=conv_ln_g.reshape(1, ch),
        ln_b=conv_ln_b.reshape(1, ch),
        w_attn_out=w_attn_out.astype(BF16), w_conv_out=w_conv_out.astype(BF16),
        w_out=w_out.astype(BF16), g2=norm2_g.reshape(1, d_model),
        w_up=w_up.astype(BF16), w_down=w_down.astype(BF16))


def _pick_tile(n, pref):
    t = min(n, pref)
    while n % t:
        t //= 2
    return t


def _layer(xp, xs, cache_k, cache_v, cache_lf, state_conv, page_table, p):
    batch, seq_len, d_model = xp.shape
    dec_batch, dec_seq, _ = xs.shape
    n_pool, page, _, _ = cache_k.shape
    hist = CONV_WIDTH - 1

    n_p = batch * seq_len
    tm = _pick_tile(seq_len, 512)
    q, k, v, lf, u, ga, gb, c_t = _in_proj(xp.reshape(n_p, d_model), p, tm=tm, seq_len=seq_len,
                                           with_cumsum=True, q_dtype=BF16)
    o = _attn_prompt(q, k, v, c_t, batch=batch, seq_len=seq_len, tq=_pick_tile(seq_len, 256))
    ac = _conv_prompt(u, p, batch=batch, seq_len=seq_len, tc=tm)
    yp = _merge_mlp(xp.reshape(n_p, d_model), o, ac, ga, gb, p, tm=tm).reshape(batch, seq_len, d_model)
    kp = k.reshape(batch, seq_len, N_HEADS, HEAD_DIM)
    vp = v.reshape(batch, seq_len, N_HEADS, HEAD_DIM)
    lp = lf.reshape(batch, seq_len, N_HEADS)
    cp = u.reshape(batch, seq_len, -1)[:, seq_len - hist:, :]

    n_s = dec_batch * dec_seq
    tms = _pick_tile(n_s, 512)
    qs, ks, vs, lfs, us, gas, gbs = _in_proj(xs.reshape(n_s, d_model), p, tm=tms, seq_len=dec_seq,
                                             with_cumsum=False, q_dtype=F32)
    r3 = lambda a: a.reshape(dec_batch, dec_seq, a.shape[-1])
    os_ = _attn_sample(r3(qs), r3(ks), r3(vs), r3(lfs),
                       cache_k.reshape(n_pool, page, ATTN_WIDTH), cache_v.reshape(n_pool, page, ATTN_WIDTH),
                       cache_lf, page_table, n_pg=_pick_tile(page_table.shape[1], 8))
    acs, new_state = _conv_sample(state_conv, r3(us), p, seqs_per_step=_pick_tile(dec_batch, 8))
    ys = _merge_mlp(xs.reshape(n_s, d_model), os_.reshape(n_s, ATTN_WIDTH), acs.reshape(n_s, -1),
                    gas, gbs, p, tm=tms).reshape(dec_batch, dec_seq, d_model)
    k_s = ks.reshape(dec_batch, dec_seq, N_HEADS, HEAD_DIM)
    v_s = vs.reshape(dec_batch, dec_seq, N_HEADS, HEAD_DIM)
    l_s = lfs.reshape(dec_batch, dec_seq, N_HEADS)
    return yp, ys, kp, vp, lp, cp, k_s, v_s, l_s, new_state


def kernel(x_prompt, x_sample, cache_k, cache_v, cache_logf, state_conv, page_table, norm1_g, w_in, b_forget, b_gate, q_norm_g, k_norm_g, w_attn_out, conv_dw_w, conv_dw_b, conv_ln_g, conv_ln_b, w_conv_out, w_out, norm2_g, w_up, w_down):
    depth = w_in.shape[0]
    xp, xs = x_prompt, x_sample
    outs = [[] for _ in range(8)]
    for l in range(depth):
        p = _prep_layer_params(norm1_g[l], w_in[l], b_forget[l], b_gate[l], q_norm_g[l], k_norm_g[l],
                               w_attn_out[l], conv_dw_w[l], conv_dw_b[l], conv_ln_g[l], conv_ln_b[l],
                               w_conv_out[l], w_out[l], norm2_g[l], w_up[l], w_down[l])
        xp, xs, *rest = _layer(xp, xs, cache_k[l], cache_v[l], cache_logf[l], state_conv[l], page_table, p)
        for acc, r in zip(outs, rest):
            acc.append(r)
    return (xp, xs) + tuple(jnp.stack(o) for o in outs)
```

```python
import functools

import jax
import jax.numpy as jnp
from jax import lax
from jax.experimental import pallas as pl
from jax.experimental.pallas import tpu as pltpu

N_HEADS = 8
HEAD_DIM = 64
ATTN_WIDTH = N_HEADS * HEAD_DIM
CONV_WIDTH = 31
EPS = 1e-6
NEG_BIG = -1e30

LANES = 128
SUBLANES = 8
HEADS_PER_VREG = LANES // HEAD_DIM
VMEM_LIMIT_BYTES = 56 * 1024 * 1024

F32 = jnp.float32
BF16 = jnp.bfloat16

NT_DIMS = (((1,), (1,)), ((), ()))


def _const_spec(shape):
    nd = len(shape)
    return pl.BlockSpec(shape, lambda *_: (0,) * nd, pipeline_mode=pl.Buffered(1))


def _split3(x):
    hi = x.astype(BF16)
    r = x - hi.astype(F32)
    mid = r.astype(BF16)
    lo = (r - mid.astype(F32)).astype(BF16)
    return hi, mid, lo


def _log_sigmoid(x):
    return jnp.minimum(x, 0.0) - jnp.log1p(jnp.exp(-jnp.abs(x)))


def _dot(a, b):
    return jnp.dot(a, b, preferred_element_type=F32)


def _dot_nt(a, b):
    return lax.dot_general(a, b, NT_DIMS, preferred_element_type=F32)


def _upper_incl(n):
    return (lax.broadcasted_iota(jnp.int32, (n, n), 0) <= lax.broadcasted_iota(jnp.int32, (n, n), 1)).astype(BF16)


def _in_proj_body(x_ref, g1_ref, w_ref, bg_ref, qg_ref, hm_ref, *rest, tiles_per_seq, feature_major):
    if feature_major:
        (wkvt_ref, kgt_ref, wft_ref, bft_ref, tri_ref,
         q_ref, kt_ref, vt_ref, lft_ref, u_ref, ga_ref, gb_ref, ct_ref, carry_ref) = rest
    else:
        wf_ref, bf_ref, kg_ref, q_ref, k_ref, v_ref, lf_ref, u_ref, ga_ref, gb_ref = rest

    x = x_ref[...]
    ms = jnp.mean(x * x, axis=-1, keepdims=True)
    h = (x * lax.rsqrt(ms + EPS) * g1_ref[...]).astype(BF16)

    A = ATTN_WIDTH
    d_model = x.shape[1]
    hm = hm_ref[...]

    def head_rms(z, g):
        msq = _dot((z * z).astype(BF16), hm)
        return z * lax.rsqrt(msq + EPS) * g

    zq = _dot(h, w_ref[:, 0:A])
    q_ref[...] = head_rms(zq, qg_ref[...]).astype(q_ref.dtype)

    if feature_major:
        zkt = _dot_nt(wkvt_ref[0:A, :], h)
        for hd in range(N_HEADS):
            rows = slice(hd * HEAD_DIM, (hd + 1) * HEAD_DIM)
            zh = zkt[rows, :]
            msq = jnp.mean(zh * zh, axis=0, keepdims=True)
            kt_ref[rows, :] = zh * lax.rsqrt(msq + EPS) * kgt_ref[rows, :]
        vt_ref[...] = _dot_nt(wkvt_ref[A:2 * A, :], h)

        @pl.when(pl.program_id(0) % tiles_per_seq == 0)
        def _():
            carry_ref[...] = jnp.zeros_like(carry_ref)

        lft = _log_sigmoid(_dot_nt(wft_ref[...], h) + bft_ref[...])
        lft_ref[...] = lft
        tri = tri_ref[...]
        c = carry_ref[:, 0:1]
        for part in _split3(lft):
            c = c + _dot(part, tri)
        ct_ref[...] = c
        carry_ref[...] = jnp.broadcast_to(c[:, -1:], carry_ref.shape)
    else:
        zk = _dot(h, w_ref[:, A:2 * A])
        k_ref[...] = head_rms(zk, kg_ref[...])
        v_ref[...] = _dot(h, w_ref[:, 2 * A:3 * A])
        lf_ref[...] = _log_sigmoid(_dot(h, wf_ref[...]) + bf_ref[...])

    za = _dot(h, w_ref[:, 3 * A:4 * A])
    zu = _dot(h, w_ref[:, 4 * A:5 * A])
    u_ref[...] = za * jax.nn.sigmoid(zu)

    o = 5 * A
    zga = _dot(h, w_ref[:, o:o + d_model])
    ga_ref[...] = jax.nn.sigmoid(zga + bg_ref[:, 0:d_model]).astype(ga_ref.dtype)
    zgb = _dot(h, w_ref[:, o + d_model:o + 2 * d_model])
    gb_ref[...] = jax.nn.sigmoid(zgb + bg_ref[:, d_model:2 * d_model]).astype(gb_ref.dtype)


def _in_proj(x2d, p, *, tm, seq_len, feature_major, q_dtype):
    n, d_model = x2d.shape
    assert n % tm == 0
    n_tiles = n // tm
    A = ATTN_WIDTH
    wcols = p["w_cat"].shape[1]

    row = lambda c: pl.BlockSpec((tm, c), lambda i: (i, 0))
    in_specs = [row(d_model), _const_spec((1, d_model)), _const_spec((d_model, wcols)),
                _const_spec((1, 2 * d_model)), _const_spec((1, A)), _const_spec((A, A))]
    args = [x2d, p["g1"], p["w_cat"], p["b_gate"], p["qg"], p["head_mean"]]
    tok = lambda c, dt: jax.ShapeDtypeStruct((n, c), dt)
    tail_shapes = [tok(A, F32), tok(d_model, BF16), tok(d_model, BF16)]
    tail_specs = [row(A), row(d_model), row(d_model)]
    scratch = []
    tiles_per_seq = 1
    if feature_major:
        assert seq_len % tm == 0
        tiles_per_seq = seq_len // tm
        batch = n // seq_len
        in_specs += [_const_spec((2 * A, d_model)), _const_spec((A, 1)), _const_spec((N_HEADS, d_model)),
                     _const_spec((N_HEADS, 1)), _const_spec((tm, tm))]
        args += [p["w_kv_t"], p["kg_t"], p["w_f_t"], p["b_f_t"], _upper_incl(tm)]
        fm = lambda r: pl.BlockSpec((None, r, tm), lambda i: (i // tiles_per_seq, 0, i % tiles_per_seq))
        fms = lambda r: jax.ShapeDtypeStruct((batch, r, seq_len), F32)
        out_shape = [tok(A, q_dtype), fms(A), fms(A), fms(N_HEADS)] + tail_shapes
        out_shape.append(jax.ShapeDtypeStruct((batch * N_HEADS, seq_len), F32))
        out_specs = [row(A), fm(A), fm(A), fm(N_HEADS)] + tail_specs
        out_specs.append(pl.BlockSpec((N_HEADS, tm), lambda i: (i // tiles_per_seq, i % tiles_per_seq)))
        scratch.append(pltpu.VMEM((N_HEADS, LANES), F32))
    else:
        in_specs += [_const_spec((d_model, N_HEADS)), _const_spec((1, N_HEADS)), _const_spec((1, A))]
        args += [p["w_f"], p["b_f"], p["kg"]]
        out_shape = [tok(A, q_dtype), tok(A, F32), tok(A, F32), tok(N_HEADS, F32)] + tail_shapes
        out_specs = [row(A), row(A), row(A), row(N_HEADS)] + tail_specs

    return pl.pallas_call(
        functools.partial(_in_proj_body, tiles_per_seq=tiles_per_seq, feature_major=feature_major),
        grid=(n_tiles,),
        in_specs=in_specs,
        out_specs=out_specs,
        out_shape=out_shape,
        scratch_shapes=scratch,
        compiler_params=pltpu.CompilerParams(dimension_semantics=("arbitrary",),
                                             vmem_limit_bytes=VMEM_LIMIT_BYTES),
        name="in_proj_fm" if feature_major else "in_proj",
    )(*args)


def _attn_prompt_body(q_ref, kt_ref, vt_ref, c_ref, o_ref, kb_ref, vb_ref, *, tq):
    hp = pl.program_id(1)
    seq = q_ref.shape[0]
    kb_ref[...] = kt_ref[...].astype(BF16)
    vb_ref[...] = vt_ref[...].astype(BF16)
    lane = lax.broadcasted_iota(jnp.int32, (1, LANES), 1)
    first_head = lane < HEAD_DIM
    causal = (lax.broadcasted_iota(jnp.int32, (tq, tq), 1)
              <= lax.broadcasted_iota(jnp.int32, (tq, tq), 0))
    c_all = c_ref[...]
    sub = lax.broadcasted_iota(jnp.int32, c_all.shape, 0)
    c_rows = [jnp.sum(jnp.where(sub == HEADS_PER_VREG * hp + hh, c_all, 0.0), axis=0, keepdims=True)
              for hh in range(HEADS_PER_VREG)]

    for i in range(seq // tq):
        q0 = i * tq
        q = q_ref[q0:q0 + tq, :]
        outs = []
        for hh in range(HEADS_PER_VREG):
            own = first_head if hh == 0 else jnp.logical_not(first_head)
            qm = jnp.where(own, q, jnp.zeros_like(q))
            c0 = c_rows[hh][:, q0:q0 + 1]
            m = jnp.full((tq, 1), NEG_BIG, F32)
            l = jnp.zeros((tq, 1), F32)
            acc = jnp.zeros((tq, LANES), F32)
            for j in range(i + 1):
                k0 = j * tq
                s = _dot(qm, kb_ref[:, k0:k0 + tq])
                s = s + (c0 - c_rows[hh][:, k0:k0 + tq])
                if j == i:
                    s = jnp.where(causal, s, NEG_BIG)
                m_new = jnp.maximum(m, jnp.max(s, axis=1, keepdims=True))
                alpha = jnp.exp(m - m_new)
                pr = jnp.exp(s - m_new)
                l = alpha * l + jnp.sum(pr, axis=1, keepdims=True)
                acc = alpha * acc + _dot_nt(pr.astype(BF16), vb_ref[:, k0:k0 + tq])
                m = m_new
            outs.append(acc * (1.0 / l))
        o_ref[q0:q0 + tq, :] = jnp.where(first_head, outs[0], outs[1]).astype(o_ref.dtype)


def _attn_prompt(q2d, kt, vt, ct, *, tq):
    batch, _, seq_len = kt.shape
    n = q2d.shape[0]
    pairs = N_HEADS // HEADS_PER_VREG
    fm = pl.BlockSpec((None, LANES, seq_len), lambda b, hp: (b, hp, 0))
    tok = pl.BlockSpec((seq_len, LANES), lambda b, hp: (b, hp))
    return pl.pallas_call(
        functools.partial(_attn_prompt_body, tq=tq),
        grid=(batch, pairs),
        in_specs=[tok, fm, fm, pl.BlockSpec((N_HEADS, seq_len), lambda b, hp: (b, 0))],
        out_specs=tok,
        out_shape=jax.ShapeDtypeStruct((n, ATTN_WIDTH), BF16),
        scratch_shapes=[pltpu.VMEM((LANES, seq_len), BF16), pltpu.VMEM((LANES, seq_len), BF16)],
        compiler_params=pltpu.CompilerParams(dimension_semantics=("arbitrary", "arbitrary"),
                                             vmem_limit_bytes=VMEM_LIMIT_BYTES),
        name="attn_prompt",
    )(q2d, kt, vt, ct)


def _attn_sample_body(pt_ref, q_ref, ks_ref, vs_ref, lfs_ref, *rest, n_pg, dec_seq):
    kp = rest[0:n_pg]
    vp = rest[n_pg:2 * n_pg]
    lp = rest[2 * n_pg:3 * n_pg]
    lincl_ref, sel_ref, uincl_ref = rest[3 * n_pg:3 * n_pg + 3]
    o_ref = rest[3 * n_pg + 3]
    qbd_ref, m_ref, l_ref, acc_ref, tot_ref, lfpad_ref, new_ref = rest[3 * n_pg + 4:]
    del pt_ref

    g = pl.program_id(1)
    rows = N_HEADS * dec_seq
    page = lfpad_ref.shape[0]
    row_head = lax.broadcasted_iota(jnp.int32, (rows, ATTN_WIDTH), 0) // dec_seq
    col_head = lax.broadcasted_iota(jnp.int32, (rows, ATTN_WIDTH), 1) // HEAD_DIM
    own_head = row_head == col_head

    def softmax_update(s, pv_fn):
        m = m_ref[...]
        m_new = jnp.maximum(m, jnp.max(s, axis=1, keepdims=True))
        alpha = jnp.exp(m - m_new)
        pr = jnp.exp(s - m_new)
        l_ref[...] = alpha * l_ref[...] + jnp.sum(pr, axis=1, keepdims=True)
        acc_ref[...] = alpha * acc_ref[...] + pv_fn(pr.astype(BF16))
        m_ref[...] = m_new

    @pl.when(g == 0)
    def _():
        q = q_ref[...]
        q_rows = jnp.concatenate([q] * N_HEADS, axis=0)
        qbd_ref[...] = jnp.where(own_head, q_rows, 0.0).astype(BF16)
        m_ref[...] = jnp.full(m_ref.shape, NEG_BIG, F32)
        l_ref[...] = jnp.zeros(l_ref.shape, F32)
        acc_ref[...] = jnp.zeros(acc_ref.shape, F32)
        tot_ref[...] = jnp.zeros(tot_ref.shape, F32)
        key = lax.broadcasted_iota(jnp.int32, (rows, page), 1)
        t = lax.broadcasted_iota(jnp.int32, (rows, page), 0) % dec_seq
        for slot, src in ((0, ks_ref), (1, vs_ref)):
            new_ref[slot] = jnp.zeros(new_ref.shape[1:], F32)
            new_ref[slot, 0:dec_seq, :] = src[...]
        k_new = new_ref[0].astype(BF16)
        v_new = new_ref[1].astype(BF16)
        lfpad_ref[...] = jnp.zeros(lfpad_ref.shape, F32)
        lfpad_ref[0:dec_seq, 0:N_HEADS] = lfs_ref[...]
        lincl = lincl_ref[...]
        pre = jnp.zeros((page, LANES), F32)
        for part in _split3(lfpad_ref[...]):
            pre = pre + _dot(lincl, part)
        sel = sel_ref[...]
        pre_rows = jnp.zeros((rows, page), F32)
        for part in _split3(pre):
            pre_rows = pre_rows + _dot_nt(sel, part)
        tot_new = pre_rows[:, page - 1:page]
        s = _dot_nt(qbd_ref[...], k_new) + (tot_new - pre_rows)
        softmax_update(jnp.where(key <= t, s, NEG_BIG), lambda pr: _dot(pr, v_new))
        tot_ref[...] = tot_new

    lft = jnp.concatenate([lp[i][...] for i in range(n_pg)], axis=0)
    uincl = uincl_ref[...]
    pre_all = jnp.zeros(lft.shape, F32)
    for part in _split3(lft):
        pre_all = pre_all + _dot(part, uincl)
    qbd = qbd_ref[...]
    run = tot_ref[...]
    scores = [None] * n_pg
    for i in reversed(range(n_pg)):
        pre_rows = jnp.concatenate(
            [jnp.broadcast_to(pre_all[i * N_HEADS + hd:i * N_HEADS + hd + 1, :], (dec_seq, page))
             for hd in range(N_HEADS)], axis=0)
        run = run + pre_rows[:, page - 1:page]
        scores[i] = _dot(qbd, kp[i][...].astype(BF16)) + (run - pre_rows)
    tot_ref[...] = run

    def pv_pages(pr):
        out = _dot_nt(pr[:, 0:page], vp[0][...].astype(BF16))
        for i in range(1, n_pg):
            out = out + _dot_nt(pr[:, i * page:(i + 1) * page], vp[i][...].astype(BF16))
        return out

    softmax_update(jnp.concatenate(scores, axis=1), pv_pages)

    @pl.when(g == pl.num_programs(1) - 1)
    def _():
        o = jnp.where(own_head, acc_ref[...] * (1.0 / l_ref[...]), 0.0)
        out = o[0:dec_seq, :]
        for hd in range(1, N_HEADS):
            out = out + o[hd * dec_seq:(hd + 1) * dec_seq, :]
        o_ref[...] = out


def _attn_sample(q3, k3, v3, lf3, cache_kt, cache_vt, cache_lft, page_table, *, n_pg):
    dec_batch, dec_seq, _ = q3.shape
    n_pool, _, page = cache_kt.shape
    n_pages = page_table.shape[1]
    assert n_pages % n_pg == 0 and dec_seq == SUBLANES and page == LANES
    steps = n_pages // n_pg
    rows = N_HEADS * dec_seq

    uincl = _upper_incl(page)
    sel = (lax.broadcasted_iota(jnp.int32, (rows, LANES), 0) // dec_seq
           == lax.broadcasted_iota(jnp.int32, (rows, LANES), 1)).astype(BF16)

    def page_map(slot):
        def index_map(b, g, pt):
            return (pt[b * n_pages + n_pages - (g + 1) * n_pg + slot], 0, 0)
        return index_map

    per_b = lambda c: pl.BlockSpec((None, dec_seq, c), lambda b, g, pt: (b, 0, 0))
    const = lambda shape: pl.BlockSpec(shape, lambda b, g, pt: (0, 0))
    in_specs = [per_b(ATTN_WIDTH), per_b(ATTN_WIDTH), per_b(ATTN_WIDTH), per_b(N_HEADS)]
    in_specs += [pl.BlockSpec((None, ATTN_WIDTH, page), page_map(i)) for i in range(n_pg)]
    in_specs += [pl.BlockSpec((None, ATTN_WIDTH, page), page_map(i)) for i in range(n_pg)]
    in_specs += [pl.BlockSpec((None, N_HEADS, page), page_map(i)) for i in range(n_pg)]
    in_specs += [const((page, page)), const((rows, LANES)), const((page, page))]
    grid_spec = pltpu.PrefetchScalarGridSpec(
        num_scalar_prefetch=1,
        grid=(dec_batch, steps),
        in_specs=in_specs,
        out_specs=pl.BlockSpec((None, dec_seq, ATTN_WIDTH), lambda b, g, pt: (b, 0, 0)),
        scratch_shapes=[pltpu.VMEM((rows, ATTN_WIDTH), BF16), pltpu.VMEM((rows, 1), F32),
                        pltpu.VMEM((rows, 1), F32), pltpu.VMEM((rows, ATTN_WIDTH), F32),
                        pltpu.VMEM((rows, 1), F32), pltpu.VMEM((page, LANES), F32),
                        pltpu.VMEM((2, page, ATTN_WIDTH), F32)],
    )
    args = [page_table.reshape(-1), q3, k3, v3, lf3]
    args += [cache_kt] * n_pg + [cache_vt] * n_pg + [cache_lft] * n_pg + [uincl.T, sel, uincl]
    return pl.pallas_call(
        functools.partial(_attn_sample_body, n_pg=n_pg, dec_seq=dec_seq),
        grid_spec=grid_spec,
        out_shape=jax.ShapeDtypeStruct((dec_batch, dec_seq, ATTN_WIDTH), F32),
        compiler_params=pltpu.CompilerParams(dimension_semantics=("arbitrary", "arbitrary"),
                                             vmem_limit_bytes=VMEM_LIMIT_BYTES),
        name="attn_sample",
    )(*args)


def _ln_swish(c, g, b):
    mu = jnp.mean(c, axis=-1, keepdims=True)
    d = c - mu
    var = jnp.mean(d * d, axis=-1, keepdims=True)
    y = d * lax.rsqrt(var + EPS) * g + b
    return y * jax.nn.sigmoid(y)


HIST_ROWS = 32
HIST_PAD = HIST_ROWS - (CONV_WIDTH - 1)
TAP_GROUPS = -(-(HIST_PAD + CONV_WIDTH) // SUBLANES)


def _conv_prompt_body(u_ref, w_ref, b_ref, g_ref, beta_ref, o_ref, ext_ref, *, rows_per_chunk):
    tc, ch = u_ref.shape
    R = rows_per_chunk

    @pl.when(pl.program_id(1) == 0)
    def _():
        ext_ref[0:HIST_ROWS, :] = jnp.zeros((HIST_ROWS, ch), F32)
        ext_ref[HIST_ROWS + tc:HIST_ROWS + tc + SUBLANES, :] = jnp.zeros((SUBLANES, ch), F32)

    ext_ref[HIST_ROWS:HIST_ROWS + tc, :] = u_ref[...]

    for r0 in range(0, tc, R):
        cols = []
        for cb in range(ch // LANES):
            lanes = slice(cb * LANES, (cb + 1) * LANES)
            y = jnp.zeros((R, LANES), F32) + b_ref[:, lanes]
            for r in range(SUBLANES):
                part = None
                for a in range(TAP_GROUPS):
                    k = SUBLANES * a + r - HIST_PAD
                    if 0 <= k < CONV_WIDTH:
                        term = ext_ref[r0 + SUBLANES * a:r0 + SUBLANES * a + R + SUBLANES, lanes] * w_ref[k:k + 1, lanes]
                        part = term if part is None else part + term
                y = y + part[r:r + R, :]
            cols.append(y)
        acc = jnp.concatenate(cols, axis=1)
        o_ref[r0:r0 + R, :] = _ln_swish(acc, g_ref[...], beta_ref[...]).astype(o_ref.dtype)
    ext_ref[0:HIST_ROWS, :] = ext_ref[tc:tc + HIST_ROWS, :]


def _conv_prompt(u2d, p, *, batch, seq_len, tc):
    n, ch = u2d.shape
    tiles = seq_len // tc
    return pl.pallas_call(
        functools.partial(_conv_prompt_body, rows_per_chunk=_pick_tile(tc, 128)),
        grid=(batch, tiles),
        in_specs=[pl.BlockSpec((tc, ch), lambda b, i: (b * tiles + i, 0)),
                  _const_spec((CONV_WIDTH, ch)), _const_spec((1, ch)), _const_spec((1, ch)),
                  _const_spec((1, ch))],
        out_specs=pl.BlockSpec((tc, ch), lambda b, i: (b * tiles + i, 0)),
        out_shape=jax.ShapeDtypeStruct((n, ch), BF16),
        scratch_shapes=[pltpu.VMEM((HIST_ROWS + tc + SUBLANES, ch), F32)],
        compiler_params=pltpu.CompilerParams(dimension_semantics=("arbitrary", "arbitrary"),
                                             vmem_limit_bytes=VMEM_LIMIT_BYTES),
        name="conv_prompt",
    )(u2d, p["conv_w"], p["conv_b"], p["ln_g"], p["ln_b"])


def _conv_sample_body(st_ref, u_ref, w_ref, b_ref, g_ref, beta_ref, o_ref, ns_ref, ext_ref):
    hist = CONV_WIDTH - 1
    dec_seq = u_ref.shape[1]

    def one_seq(sb, _):
        ext_ref[0:hist, :] = st_ref[sb]
        ext_ref[hist:hist + dec_seq, :] = u_ref[sb]
        acc = jnp.zeros((dec_seq, ext_ref.shape[1]), F32) + b_ref[...]
        for k in range(CONV_WIDTH):
            acc = acc + ext_ref[k:k + dec_seq, :] * w_ref[k:k + 1, :]
        o_ref[sb] = _ln_swish(acc, g_ref[...], beta_ref[...]).astype(o_ref.dtype)
        ns_ref[sb] = ext_ref[dec_seq:dec_seq + hist, :]
        return 0

    lax.fori_loop(0, u_ref.shape[0], one_seq, 0)


def _conv_sample(state, u3, p, *, seqs_per_step):
    dec_batch, dec_seq, ch = u3.shape
    hist = CONV_WIDTH - 1
    blk = lambda r: pl.BlockSpec((seqs_per_step, r, ch), lambda i: (i, 0, 0))
    return pl.pallas_call(
        _conv_sample_body,
        grid=(dec_batch // seqs_per_step,),
        in_specs=[blk(hist), blk(dec_seq), _const_spec((CONV_WIDTH, ch)), _const_spec((1, ch)),
                  _const_spec((1, ch)), _const_spec((1, ch))],
        out_specs=[blk(dec_seq), blk(hist)],
        out_shape=[jax.ShapeDtypeStruct((dec_batch, dec_seq, ch), F32),
                   jax.ShapeDtypeStruct((dec_batch, hist, ch), F32)],
        scratch_shapes=[pltpu.VMEM((hist + dec_seq + 2, ch), F32)],
        compiler_params=pltpu.CompilerParams(dimension_semantics=("arbitrary",),
                                             vmem_limit_bytes=VMEM_LIMIT_BYTES),
        name="conv_sample",
    )(state, u3, p["conv_w"], p["conv_b"], p["ln_g"], p["ln_b"])


def _merge_mlp_body(x_ref, o_ref, ac_ref, ga_ref, gb_ref, wao_ref, wco_ref, wout_ref, g2_ref,
                    wup_ref, wdn_ref, y_ref, *, ff_chunk):
    ya = _dot(o_ref[...].astype(BF16), wao_ref[...])
    yc = _dot(ac_ref[...].astype(BF16), wco_ref[...])
    mix = ga_ref[...].astype(F32) * ya + gb_ref[...].astype(F32) * yc
    x1 = x_ref[...] + _dot(mix.astype(BF16), wout_ref[...])
    ms = jnp.mean(x1 * x1, axis=-1, keepdims=True)
    h2 = (x1 * lax.rsqrt(ms + EPS) * g2_ref[...]).astype(BF16)
    d_ff = wup_ref.shape[1]
    acc = x1
    for c in range(d_ff // ff_chunk):
        a = jnp.maximum(_dot(h2, wup_ref[:, c * ff_chunk:(c + 1) * ff_chunk]), 0.0)
        acc = acc + _dot((a * a).astype(BF16), wdn_ref[c * ff_chunk:(c + 1) * ff_chunk, :])
    y_ref[...] = acc


def _merge_mlp(x2d, o2d, ac2d, ga, gb, p, *, tm):
    n, d_model = x2d.shape
    d_ff = p["w_up"].shape[1]
    ch = ac2d.shape[1]
    row = lambda c: pl.BlockSpec((tm, c), lambda i: (i, 0))
    return pl.pallas_call(
        functools.partial(_merge_mlp_body, ff_chunk=512),
        grid=(n // tm,),
        in_specs=[row(d_model), row(ATTN_WIDTH), row(ch), row(d_model), row(d_model),
                  _const_spec((ATTN_WIDTH, d_model)), _const_spec((ch, d_model)),
                  _const_spec((d_model, d_model)), _const_spec((1, d_model)),
                  _const_spec((d_model, d_ff)), _const_spec((d_ff, d_model))],
        out_specs=row(d_model),
        out_shape=jax.ShapeDtypeStruct((n, d_model), F32),
        compiler_params=pltpu.CompilerParams(dimension_semantics=("arbitrary",),
                                             vmem_limit_bytes=VMEM_LIMIT_BYTES),
        name="merge_mlp",
    )(x2d, o2d, ac2d, ga, gb, p["w_attn_out"], p["w_conv_out"], p["w_out"], p["g2"], p["w_up"], p["w_down"])


def _prep_layer_params(norm1_g, w_in, b_forget, b_gate, q_norm_g, k_norm_g, w_attn_out, conv_dw_w,
                       conv_dw_b, conv_ln_g, conv_ln_b, w_conv_out, w_out, norm2_g, w_up, w_down):
    A = ATTN_WIDTH
    d_model = w_in.shape[0]
    ch = conv_dw_w.shape[1]
    f0 = 3 * A
    a0 = f0 + N_HEADS
    w_bf = w_in.astype(BF16)
    w_f = w_bf[:, f0:a0]
    w_cat = jnp.concatenate([w_bf[:, 0:f0], w_bf[:, a0:]], axis=1)
    hm = (lax.broadcasted_iota(jnp.int32, (A, A), 0) // HEAD_DIM
          == lax.broadcasted_iota(jnp.int32, (A, A), 1) // HEAD_DIM).astype(BF16) * (1.0 / HEAD_DIM)
    kg = jnp.tile(k_norm_g, N_HEADS)
    return dict(
        g1=norm1_g.reshape(1, d_model), w_cat=w_cat, w_f=w_f, w_f_t=w_f.T, w_kv_t=w_bf[:, A:f0].T,
        b_f=b_forget.reshape(1, N_HEADS), b_f_t=b_forget.reshape(N_HEADS, 1),
        b_gate=b_gate.reshape(1, 2 * d_model),
        qg=jnp.tile(q_norm_g, N_HEADS).reshape(1, A) * (HEAD_DIM ** -0.5),
        kg=kg.reshape(1, A), kg_t=kg.reshape(A, 1),
        head_mean=hm.astype(BF16),
        conv_w=conv_dw_w, conv_b=conv_dw_b.reshape(1, ch), ln_g=conv_ln_g.reshape(1, ch),
        ln_b=conv_ln_b.reshape(1, ch),
        w_attn_out=w_attn_out.astype(BF16), w_conv_out=w_conv_out.astype(BF16),
        w_out=w_out.astype(BF16), g2=norm2_g.reshape(1, d_model),
        w_up=w_up.astype(BF16), w_down=w_down.astype(BF16))


def _pick_tile(n, pref):
    t = min(n, pref)
    while n % t:
        t //= 2
    return t


def _layer(xp, xs, cache_k, cache_v, cache_lf, state_conv, page_table, p):
    batch, seq_len, d_model = xp.shape
    dec_batch, dec_seq, _ = xs.shape
    n_pool, page, _, _ = cache_k.shape
    hist = CONV_WIDTH - 1

    n_p = batch * seq_len
    tm = _pick_tile(seq_len, 512)
    q, kt, vt, lft, u, ga, gb, ct = _in_proj(xp.reshape(n_p, d_model), p, tm=tm, seq_len=seq_len,
                                             feature_major=True, q_dtype=BF16)
    o = _attn_prompt(q, kt, vt, ct, tq=_pick_tile(seq_len, 512))
    ac = _conv_prompt(u, p, batch=batch, seq_len=seq_len, tc=tm)
    yp = _merge_mlp(xp.reshape(n_p, d_model), o, ac, ga, gb, p, tm=tm).reshape(batch, seq_len, d_model)
    to_tok = lambda a: jnp.transpose(a.reshape(batch, N_HEADS, HEAD_DIM, seq_len), (0, 3, 1, 2))
    kp, vp = to_tok(kt), to_tok(vt)
    lp = jnp.transpose(lft, (0, 2, 1))
    cp = u.reshape(batch, seq_len, -1)[:, seq_len - hist:, :]

    n_s = dec_batch * dec_seq
    tms = _pick_tile(n_s, 512)
    qs, ks, vs, lfs, us, gas, gbs = _in_proj(xs.reshape(n_s, d_model), p, tm=tms, seq_len=dec_seq,
                                             feature_major=False, q_dtype=F32)
    r3 = lambda a: a.reshape(dec_batch, dec_seq, a.shape[-1])
    fm_cache = lambda c: jnp.transpose(c, (0, 2, 3, 1)).reshape(n_pool, ATTN_WIDTH, page)
    os_ = _attn_sample(r3(qs), r3(ks), r3(vs), r3(lfs), fm_cache(cache_k), fm_cache(cache_v),
                       jnp.transpose(cache_lf, (0, 2, 1)), page_table,
                       n_pg=_pick_tile(page_table.shape[1], 8))
    acs, new_state = _conv_sample(state_conv, r3(us), p, seqs_per_step=_pick_tile(dec_batch, 8))
    ys = _merge_mlp(xs.reshape(n_s, d_model), os_.reshape(n_s, ATTN_WIDTH), acs.reshape(n_s, -1),
                    gas, gbs, p, tm=tms).reshape(dec_batch, dec_seq, d_model)
    k_s = ks.reshape(dec_batch, dec_seq, N_HEADS, HEAD_DIM)
    v_s = vs.reshape(dec_batch, dec_seq, N_HEADS, HEAD_DIM)
    l_s = lfs.reshape(dec_batch, dec_seq, N_HEADS)
    return yp, ys, kp, vp, lp, cp, k_s, v_s, l_s, new_state


def kernel(x_prompt, x_sample, cache_k, cache_v, cache_logf, state_conv, page_table, norm1_g, w_in, b_forget, b_gate, q_norm_g, k_norm_g, w_attn_out, conv_dw_w, conv_dw_b, conv_ln_g, conv_ln_b, w_conv_out, w_out, norm2_g, w_up, w_down):
    depth = w_in.shape[0]
    xp, xs = x_prompt, x_sample
    outs = [[] for _ in range(8)]
    for l in range(depth):
        p = _prep_layer_params(norm1_g[l], w_in[l], b_forget[l], b_gate[l], q_norm_g[l], k_norm_g[l],
                               w_attn_out[l], conv_dw_w[l], conv_dw_b[l], conv_ln_g[l], conv_ln_b[l],
                               w_conv_out[l], w_out[l], norm2_g[l], w_up[l], w_down[l])
        xp, xs, *rest = _layer(xp, xs, cache_k[l], cache_v[l], cache_logf[l], state_conv[l], page_table, p)
        for acc, r in zip(outs, rest):
            acc.append(r)
    return (xp, xs) + tuple(jnp.stack(o) for o in outs)
```

```python
import functools

import jax
import jax.numpy as jnp
from jax import lax
from jax.experimental import pallas as pl
from jax.experimental.pallas import tpu as pltpu

N_HEADS = 8
HEAD_DIM = 64
ATTN_WIDTH = N_HEADS * HEAD_DIM
CONV_WIDTH = 31
EPS = 1e-6
NEG_BIG = -1e30
LOG2E = 1.4426950408889634

LANES = 128
SUBLANES = 8
HEADS_PER_VREG = LANES // HEAD_DIM
IN_PROJ_SUB_ROWS = 512
VMEM_LIMIT_BYTES = 56 * 1024 * 1024

F32 = jnp.float32
BF16 = jnp.bfloat16

NT_DIMS = (((1,), (1,)), ((), ()))


def _const_spec(shape):
    nd = len(shape)
    return pl.BlockSpec(shape, lambda *_: (0,) * nd, pipeline_mode=pl.Buffered(1))


def _split3(x):
    hi = x.astype(BF16)
    r = x - hi.astype(F32)
    mid = r.astype(BF16)
    lo = (r - mid.astype(F32)).astype(BF16)
    return hi, mid, lo


def _log_sigmoid(x):
    return jnp.minimum(x, 0.0) - jnp.log1p(jnp.exp(-jnp.abs(x)))


def _dot(a, b):
    return jnp.dot(a, b, preferred_element_type=F32)


def _dot_nt(a, b):
    return lax.dot_general(a, b, NT_DIMS, preferred_element_type=F32)


def _upper_incl(n):
    return (lax.broadcasted_iota(jnp.int32, (n, n), 0) <= lax.broadcasted_iota(jnp.int32, (n, n), 1)).astype(BF16)


def _in_proj_body(x_ref, g1_ref, w_ref, bg_ref, qg_ref, hm_ref, *rest, tiles_per_seq, feature_major, sub_rows):
    if feature_major:
        (wkvt_ref, kgt_ref, wft_ref, bft_ref, tri_ref,
         q_ref, kt_ref, vt_ref, lft_ref, u_ref, ga_ref, gb_ref, ct_ref, carry_ref) = rest

        @pl.when(pl.program_id(0) % tiles_per_seq == 0)
        def _():
            carry_ref[...] = jnp.zeros_like(carry_ref)

        carry = carry_ref[:, 0:1]
    else:
        wf_ref, bf_ref, kg_ref, q_ref, k_ref, v_ref, lf_ref, u_ref, ga_ref, gb_ref = rest

    A = ATTN_WIDTH
    tm, d_model = x_ref.shape
    hm = hm_ref[...]

    def head_rms(z, g):
        msq = _dot((z * z).astype(BF16), hm)
        return z * lax.rsqrt(msq + EPS) * g

    for r0 in range(0, tm, sub_rows):
        rs = slice(r0, r0 + sub_rows)
        x = x_ref[rs, :]
        ms = jnp.mean(x * x, axis=-1, keepdims=True)
        h = (x * lax.rsqrt(ms + EPS) * g1_ref[...]).astype(BF16)

        zq = _dot(h, w_ref[:, 0:A])
        q_ref[rs, :] = head_rms(zq, qg_ref[...]).astype(q_ref.dtype)

        if feature_major:
            zkt = _dot_nt(wkvt_ref[0:A, :], h)
            for hd in range(N_HEADS):
                rows = slice(hd * HEAD_DIM, (hd + 1) * HEAD_DIM)
                zh = zkt[rows, :]
                msq = jnp.mean(zh * zh, axis=0, keepdims=True)
                kt_ref[rows, rs] = zh * lax.rsqrt(msq + EPS) * kgt_ref[rows, :]
            vt_ref[:, rs] = _dot_nt(wkvt_ref[A:2 * A, :], h)

            lft = _log_sigmoid(_dot_nt(wft_ref[...], h) + bft_ref[...])
            lft_ref[:, rs] = lft
            tri = tri_ref[...]
            c = carry
            for part in _split3(lft):
                c = c + _dot(part, tri)
            ct_ref[:, rs] = c
            carry = c[:, sub_rows - 1:sub_rows]
        else:
            zk = _dot(h, w_ref[:, A:2 * A])
            k_ref[rs, :] = head_rms(zk, kg_ref[...])
            v_ref[rs, :] = _dot(h, w_ref[:, 2 * A:3 * A])
            lf_ref[rs, :] = _log_sigmoid(_dot(h, wf_ref[...]) + bf_ref[...])

        za = _dot(h, w_ref[:, 3 * A:4 * A])
        zu = _dot(h, w_ref[:, 4 * A:5 * A])
        u_ref[rs, :] = za * jax.nn.sigmoid(zu)

        o = 5 * A
        zga = _dot(h, w_ref[:, o:o + d_model])
        ga_ref[rs, :] = jax.nn.sigmoid(zga + bg_ref[:, 0:d_model]).astype(ga_ref.dtype)
        zgb = _dot(h, w_ref[:, o + d_model:o + 2 * d_model])
        gb_ref[rs, :] = jax.nn.sigmoid(zgb + bg_ref[:, d_model:2 * d_model]).astype(gb_ref.dtype)

    if feature_major:
        carry_ref[...] = jnp.broadcast_to(carry, carry_ref.shape)


def _in_proj(x2d, p, *, tm, seq_len, feature_major, q_dtype):
    n, d_model = x2d.shape
    assert n % tm == 0
    n_tiles = n // tm
    sub_rows = _pick_tile(tm, IN_PROJ_SUB_ROWS)
    A = ATTN_WIDTH
    wcols = p["w_cat"].shape[1]

    row = lambda c: pl.BlockSpec((tm, c), lambda i: (i, 0))
    in_specs = [row(d_model), _const_spec((1, d_model)), _const_spec((d_model, wcols)),
                _const_spec((1, 2 * d_model)), _const_spec((1, A)), _const_spec((A, A))]
    args = [x2d, p["g1"], p["w_cat"], p["b_gate"], p["qg_log2"] if feature_major else p["qg"], p["head_mean"]]
    tok = lambda c, dt: jax.ShapeDtypeStruct((n, c), dt)
    tail_shapes = [tok(A, F32), tok(d_model, BF16), tok(d_model, BF16)]
    tail_specs = [row(A), row(d_model), row(d_model)]
    scratch = []
    tiles_per_seq = 1
    if feature_major:
        assert seq_len % tm == 0
        tiles_per_seq = seq_len // tm
        batch = n // seq_len
        in_specs += [_const_spec((2 * A, d_model)), _const_spec((A, 1)), _const_spec((N_HEADS, d_model)),
                     _const_spec((N_HEADS, 1)), _const_spec((sub_rows, sub_rows))]
        args += [p["w_kv_t"], p["kg_t"], p["w_f_t"], p["b_f_t"], _upper_incl(sub_rows)]
        fm = lambda r: pl.BlockSpec((None, r, tm), lambda i: (i // tiles_per_seq, 0, i % tiles_per_seq))
        fms = lambda r: jax.ShapeDtypeStruct((batch, r, seq_len), F32)
        out_shape = [tok(A, q_dtype), fms(A), fms(A), fms(N_HEADS)] + tail_shapes
        out_shape.append(jax.ShapeDtypeStruct((batch * N_HEADS, seq_len), F32))
        out_specs = [row(A), fm(A), fm(A), fm(N_HEADS)] + tail_specs
        out_specs.append(pl.BlockSpec((N_HEADS, tm), lambda i: (i // tiles_per_seq, i % tiles_per_seq)))
        scratch.append(pltpu.VMEM((N_HEADS, LANES), F32))
    else:
        in_specs += [_const_spec((d_model, N_HEADS)), _const_spec((1, N_HEADS)), _const_spec((1, A))]
        args += [p["w_f"], p["b_f"], p["kg"]]
        out_shape = [tok(A, q_dtype), tok(A, F32), tok(A, F32), tok(N_HEADS, F32)] + tail_shapes
        out_specs = [row(A), row(A), row(A), row(N_HEADS)] + tail_specs

    return pl.pallas_call(
        functools.partial(_in_proj_body, tiles_per_seq=tiles_per_seq, feature_major=feature_major,
                          sub_rows=sub_rows),
        grid=(n_tiles,),
        in_specs=in_specs,
        out_specs=out_specs,
        out_shape=out_shape,
        scratch_shapes=scratch,
        compiler_params=pltpu.CompilerParams(dimension_semantics=("arbitrary",),
                                             vmem_limit_bytes=VMEM_LIMIT_BYTES),
        name="in_proj_fm" if feature_major else "in_proj",
    )(*args)


def _attn_prompt_body(q_ref, kt_ref, vt_ref, c_ref, o_ref, kb_ref, vb_ref, *, tq):
    hp = pl.program_id(1)
    seq = q_ref.shape[0]
    vb_ref[...] = vt_ref[...].astype(BF16)
    lane = lax.broadcasted_iota(jnp.int32, (1, LANES), 1)
    first_head = lane < HEAD_DIM
    causal = (lax.broadcasted_iota(jnp.int32, (tq, tq), 1)
              <= lax.broadcasted_iota(jnp.int32, (tq, tq), 0))
    kt = kt_ref[...]
    c_all = c_ref[...] * LOG2E
    sub = lax.broadcasted_iota(jnp.int32, c_all.shape, 0)
    q_fill = []
    for hh in range(HEADS_PER_VREG):
        c_row = jnp.sum(jnp.where(sub == HEADS_PER_VREG * hp + hh, c_all, 0.0), axis=0, keepdims=True)
        hi, mid, lo = (part.astype(F32) for part in _split3(-c_row))
        base = (1 - hh) * HEAD_DIM
        blk = kt[base:base + SUBLANES, :]
        blk = jnp.where(sub == 0, hi, jnp.where(sub == 1, mid, jnp.where(sub == 2, lo, blk)))
        pieces = [kt[0:base, :]] * (base > 0) + [blk, kt[base + SUBLANES:, :]]
        kb_ref[hh] = jnp.concatenate(pieces, axis=0).astype(BF16)
        q_fill.append(jnp.where((lane >= base) & (lane < base + 3), 1.0, 0.0).astype(BF16))

    for i in range(seq // tq):
        q0 = i * tq
        q = q_ref[q0:q0 + tq, :]
        outs = []
        for hh in range(HEADS_PER_VREG):
            own = first_head if hh == 0 else jnp.logical_not(first_head)
            qm = jnp.where(own, q, q_fill[hh])
            m = jnp.full((tq, 1), NEG_BIG, F32)
            l = jnp.zeros((tq, 1), F32)
            acc = jnp.zeros((tq, LANES), F32)
            for j in range(i + 1):
                k0 = j * tq
                s = _dot(qm, kb_ref[hh, :, k0:k0 + tq])
                if j == i:
                    s = jnp.where(causal, s, NEG_BIG)
                m_new = jnp.maximum(m, jnp.max(s, axis=1, keepdims=True))
                alpha = jnp.exp2(m - m_new)
                pr = jnp.exp2(s - m_new)
                l = alpha * l + jnp.sum(pr, axis=1, keepdims=True)
                acc = alpha * acc + _dot_nt(pr.astype(BF16), vb_ref[:, k0:k0 + tq])
                m = m_new
            outs.append(acc * (1.0 / l))
        o_ref[q0:q0 + tq, :] = jnp.where(first_head, outs[0], outs[1]).astype(o_ref.dtype)


def _attn_prompt(q2d, kt, vt, ct, *, tq):
    batch, _, seq_len = kt.shape
    n = q2d.shape[0]
    pairs = N_HEADS // HEADS_PER_VREG
    fm = pl.BlockSpec((None, LANES, seq_len), lambda b, hp: (b, hp, 0))
    tok = pl.BlockSpec((seq_len, LANES), lambda b, hp: (b, hp))
    return pl.pallas_call(
        functools.partial(_attn_prompt_body, tq=tq),
        grid=(batch, pairs),
        in_specs=[tok, fm, fm, pl.BlockSpec((N_HEADS, seq_len), lambda b, hp: (b, 0))],
        out_specs=tok,
        out_shape=jax.ShapeDtypeStruct((n, ATTN_WIDTH), BF16),
        scratch_shapes=[pltpu.VMEM((HEADS_PER_VREG, LANES, seq_len), BF16), pltpu.VMEM((LANES, seq_len), BF16)],
        compiler_params=pltpu.CompilerParams(dimension_semantics=("arbitrary", "arbitrary"),
                                             vmem_limit_bytes=VMEM_LIMIT_BYTES),
        name="attn_prompt",
    )(q2d, kt, vt, ct)


def _attn_sample_body(pt_ref, q_ref, ks_ref, vs_ref, lfs_ref, *rest, n_pg, dec_seq):
    kp = rest[0:n_pg]
    vp = rest[n_pg:2 * n_pg]
    lp = rest[2 * n_pg:3 * n_pg]
    lincl_ref, sel_ref, uincl_ref = rest[3 * n_pg:3 * n_pg + 3]
    o_ref = rest[3 * n_pg + 3]
    qbd_ref, m_ref, l_ref, acc_ref, tot_ref, lfpad_ref, new_ref = rest[3 * n_pg + 4:]
    del pt_ref

    g = pl.program_id(1)
    rows = N_HEADS * dec_seq
    page = lfpad_ref.shape[0]
    row_head = lax.broadcasted_iota(jnp.int32, (rows, ATTN_WIDTH), 0) // dec_seq
    col_head = lax.broadcasted_iota(jnp.int32, (rows, ATTN_WIDTH), 1) // HEAD_DIM
    own_head = row_head == col_head

    def softmax_update(s, pv_fn):
        m = m_ref[...]
        m_new = jnp.maximum(m, jnp.max(s, axis=1, keepdims=True))
        alpha = jnp.exp(m - m_new)
        pr = jnp.exp(s - m_new)
        l_ref[...] = alpha * l_ref[...] + jnp.sum(pr, axis=1, keepdims=True)
        acc_ref[...] = alpha * acc_ref[...] + pv_fn(pr.astype(BF16))
        m_ref[...] = m_new

    @pl.when(g == 0)
    def _():
        q = q_ref[...]
        q_rows = jnp.concatenate([q] * N_HEADS, axis=0)
        qbd_ref[...] = jnp.where(own_head, q_rows, 0.0).astype(BF16)
        m_ref[...] = jnp.full(m_ref.shape, NEG_BIG, F32)
        l_ref[...] = jnp.zeros(l_ref.shape, F32)
        acc_ref[...] = jnp.zeros(acc_ref.shape, F32)
        tot_ref[...] = jnp.zeros(tot_ref.shape, F32)
        key = lax.broadcasted_iota(jnp.int32, (rows, page), 1)
        t = lax.broadcasted_iota(jnp.int32, (rows, page), 0) % dec_seq
        for slot, src in ((0, ks_ref), (1, vs_ref)):
            new_ref[slot] = jnp.zeros(new_ref.shape[1:], F32)
            new_ref[slot, 0:dec_seq, :] = src[...]
        k_new = new_ref[0].astype(BF16)
        v_new = new_ref[1].astype(BF16)
        lfpad_ref[...] = jnp.zeros(lfpad_ref.shape, F32)
        lfpad_ref[0:dec_seq, 0:N_HEADS] = lfs_ref[...]
        lincl = lincl_ref[...]
        pre = jnp.zeros((page, LANES), F32)
        for part in _split3(lfpad_ref[...]):
            pre = pre + _dot(lincl, part)
        sel = sel_ref[...]
        pre_rows = jnp.zeros((rows, page), F32)
        for part in _split3(pre):
            pre_rows = pre_rows + _dot_nt(sel, part)
        tot_new = pre_rows[:, page - 1:page]
        s = _dot_nt(qbd_ref[...], k_new) + (tot_new - pre_rows)
        softmax_update(jnp.where(key <= t, s, NEG_BIG), lambda pr: _dot(pr, v_new))
        tot_ref[...] = tot_new

    lft = jnp.concatenate([lp[i][...] for i in range(n_pg)], axis=0)
    uincl = uincl_ref[...]
    pre_all = jnp.zeros(lft.shape, F32)
    for part in _split3(lft):
        pre_all = pre_all + _dot(part, uincl)
    qbd = qbd_ref[...]
    run = tot_ref[...]
    scores = [None] * n_pg
    for i in reversed(range(n_pg)):
        pre_rows = jnp.concatenate(
            [jnp.broadcast_to(pre_all[i * N_HEADS + hd:i * N_HEADS + hd + 1, :], (dec_seq, page))
             for hd in range(N_HEADS)], axis=0)
        run = run + pre_rows[:, page - 1:page]
        scores[i] = _dot(qbd, kp[i][...].astype(BF16)) + (run - pre_rows)
    tot_ref[...] = run

    def pv_pages(pr):
        out = _dot_nt(pr[:, 0:page], vp[0][...].astype(BF16))
        for i in range(1, n_pg):
            out = out + _dot_nt(pr[:, i * page:(i + 1) * page], vp[i][...].astype(BF16))
        return out

    softmax_update(jnp.concatenate(scores, axis=1), pv_pages)

    @pl.when(g == pl.num_programs(1) - 1)
    def _():
        o = jnp.where(own_head, acc_ref[...] * (1.0 / l_ref[...]), 0.0)
        out = o[0:dec_seq, :]
        for hd in range(1, N_HEADS):
            out = out + o[hd * dec_seq:(hd + 1) * dec_seq, :]
        o_ref[...] = out


def _attn_sample(q3, k3, v3, lf3, cache_kt, cache_vt, cache_lft, page_table, *, n_pg):
    dec_batch, dec_seq, _ = q3.shape
    n_pool, _, page = cache_kt.shape
    n_pages = page_table.shape[1]
    assert n_pages % n_pg == 0 and dec_seq == SUBLANES and page == LANES
    steps = n_pages // n_pg
    rows = N_HEADS * dec_seq

    uincl = _upper_incl(page)
    sel = (lax.broadcasted_iota(jnp.int32, (rows, LANES), 0) // dec_seq
           == lax.broadcasted_iota(jnp.int32, (rows, LANES), 1)).astype(BF16)

    def page_map(slot):
        def index_map(b, g, pt):
            return (pt[b * n_pages + n_pages - (g + 1) * n_pg + slot], 0, 0)
        return index_map

    per_b = lambda c: pl.BlockSpec((None, dec_seq, c), lambda b, g, pt: (b, 0, 0))
    const = lambda shape: pl.BlockSpec(shape, lambda b, g, pt: (0, 0))
    in_specs = [per_b(ATTN_WIDTH), per_b(ATTN_WIDTH), per_b(ATTN_WIDTH), per_b(N_HEADS)]
    in_specs += [pl.BlockSpec((None, ATTN_WIDTH, page), page_map(i)) for i in range(n_pg)]
    in_specs += [pl.BlockSpec((None, ATTN_WIDTH, page), page_map(i)) for i in range(n_pg)]
    in_specs += [pl.BlockSpec((None, N_HEADS, page), page_map(i)) for i in range(n_pg)]
    in_specs += [const((page, page)), const((rows, LANES)), const((page, page))]
    grid_spec = pltpu.PrefetchScalarGridSpec(
        num_scalar_prefetch=1,
        grid=(dec_batch, steps),
        in_specs=in_specs,
        out_specs=pl.BlockSpec((None, dec_seq, ATTN_WIDTH), lambda b, g, pt: (b, 0, 0)),
        scratch_shapes=[pltpu.VMEM((rows, ATTN_WIDTH), BF16), pltpu.VMEM((rows, 1), F32),
                        pltpu.VMEM((rows, 1), F32), pltpu.VMEM((rows, ATTN_WIDTH), F32),
                        pltpu.VMEM((rows, 1), F32), pltpu.VMEM((page, LANES), F32),
                        pltpu.VMEM((2, page, ATTN_WIDTH), F32)],
    )
    args = [page_table.reshape(-1), q3, k3, v3, lf3]
    args += [cache_kt] * n_pg + [cache_vt] * n_pg + [cache_lft] * n_pg + [uincl.T, sel, uincl]
    return pl.pallas_call(
        functools.partial(_attn_sample_body, n_pg=n_pg, dec_seq=dec_seq),
        grid_spec=grid_spec,
        out_shape=jax.ShapeDtypeStruct((dec_batch, dec_seq, ATTN_WIDTH), F32),
        compiler_params=pltpu.CompilerParams(dimension_semantics=("arbitrary", "arbitrary"),
                                             vmem_limit_bytes=VMEM_LIMIT_BYTES),
        name="attn_sample",
    )(*args)


def _ln_swish(c, g, b):
    mu = jnp.mean(c, axis=-1, keepdims=True)
    d = c - mu
    var = jnp.mean(d * d, axis=-1, keepdims=True)
    y = d * lax.rsqrt(var + EPS) * g + b
    return y * jax.nn.sigmoid(y)


HIST_ROWS = 32
HIST_PAD = HIST_ROWS - (CONV_WIDTH - 1)
TAP_GROUPS = -(-(HIST_PAD + CONV_WIDTH) // SUBLANES)


def _conv_prompt_body(u_ref, w_ref, b_ref, g_ref, beta_ref, o_ref, ext_ref, *, rows_per_chunk):
    tc, ch = u_ref.shape
    R = rows_per_chunk

    @pl.when(pl.program_id(1) == 0)
    def _():
        ext_ref[0:HIST_ROWS, :] = jnp.zeros((HIST_ROWS, ch), F32)
        ext_ref[HIST_ROWS + tc:HIST_ROWS + tc + SUBLANES, :] = jnp.zeros((SUBLANES, ch), F32)

    ext_ref[HIST_ROWS:HIST_ROWS + tc, :] = u_ref[...]

    for r0 in range(0, tc, R):
        cols = []
        for cb in range(ch // LANES):
            lanes = slice(cb * LANES, (cb + 1) * LANES)
            y = jnp.zeros((R, LANES), F32) + b_ref[:, lanes]
            for r in range(SUBLANES):
                part = None
                for a in range(TAP_GROUPS):
                    k = SUBLANES * a + r - HIST_PAD
                    if 0 <= k < CONV_WIDTH:
                        term = ext_ref[r0 + SUBLANES * a:r0 + SUBLANES * a + R + SUBLANES, lanes] * w_ref[k:k + 1, lanes]
                        part = term if part is None else part + term
                y = y + part[r:r + R, :]
            cols.append(y)
        acc = jnp.concatenate(cols, axis=1)
        o_ref[r0:r0 + R, :] = _ln_swish(acc, g_ref[...], beta_ref[...]).astype(o_ref.dtype)
    ext_ref[0:HIST_ROWS, :] = ext_ref[tc:tc + HIST_ROWS, :]


def _conv_prompt(u2d, p, *, batch, seq_len, tc):
    n, ch = u2d.shape
    tiles = seq_len // tc
    return pl.pallas_call(
        functools.partial(_conv_prompt_body, rows_per_chunk=_pick_tile(tc, 128)),
        grid=(batch, tiles),
        in_specs=[pl.BlockSpec((tc, ch), lambda b, i: (b * tiles + i, 0)),
                  _const_spec((CONV_WIDTH, ch)), _const_spec((1, ch)), _const_spec((1, ch)),
                  _const_spec((1, ch))],
        out_specs=pl.BlockSpec((tc, ch), lambda b, i: (b * tiles + i, 0)),
        out_shape=jax.ShapeDtypeStruct((n, ch), BF16),
        scratch_shapes=[pltpu.VMEM((HIST_ROWS + tc + SUBLANES, ch), F32)],
        compiler_params=pltpu.CompilerParams(dimension_semantics=("arbitrary", "arbitrary"),
                                             vmem_limit_bytes=VMEM_LIMIT_BYTES),
        name="conv_prompt",
    )(u2d, p["conv_w"], p["conv_b"], p["ln_g"], p["ln_b"])


def _conv_sample_body(st_ref, u_ref, w_ref, b_ref, g_ref, beta_ref, o_ref, ns_ref, ext_ref):
    hist = CONV_WIDTH - 1
    dec_seq = u_ref.shape[1]

    def one_seq(sb, _):
        ext_ref[0:hist, :] = st_ref[sb]
        ext_ref[hist:hist + dec_seq, :] = u_ref[sb]
        acc = jnp.zeros((dec_seq, ext_ref.shape[1]), F32) + b_ref[...]
        for k in range(CONV_WIDTH):
            acc = acc + ext_ref[k:k + dec_seq, :] * w_ref[k:k + 1, :]
        o_ref[sb] = _ln_swish(acc, g_ref[...], beta_ref[...]).astype(o_ref.dtype)
        ns_ref[sb] = ext_ref[dec_seq:dec_seq + hist, :]
        return 0

    lax.fori_loop(0, u_ref.shape[0], one_seq, 0)


def _conv_sample(state, u3, p, *, seqs_per_step):
    dec_batch, dec_seq, ch = u3.shape
    hist = CONV_WIDTH - 1
    blk = lambda r: pl.BlockSpec((seqs_per_step, r, ch), lambda i: (i, 0, 0))
    return pl.pallas_call(
        _conv_sample_body,
        grid=(dec_batch // seqs_per_step,),
        in_specs=[blk(hist), blk(dec_seq), _const_spec((CONV_WIDTH, ch)), _const_spec((1, ch)),
                  _const_spec((1, ch)), _const_spec((1, ch))],
        out_specs=[blk(dec_seq), blk(hist)],
        out_shape=[jax.ShapeDtypeStruct((dec_batch, dec_seq, ch), F32),
                   jax.ShapeDtypeStruct((dec_batch, hist, ch), F32)],
        scratch_shapes=[pltpu.VMEM((hist + dec_seq + 2, ch), F32)],
        compiler_params=pltpu.CompilerParams(dimension_semantics=("arbitrary",),
                                             vmem_limit_bytes=VMEM_LIMIT_BYTES),
        name="conv_sample",
    )(state, u3, p["conv_w"], p["conv_b"], p["ln_g"], p["ln_b"])


def _merge_mlp_body(x_ref, o_ref, ac_ref, ga_ref, gb_ref, wao_ref, wco_ref, wout_ref, g2_ref,
                    wup_ref, wdn_ref, y_ref, *, ff_chunk):
    ya = _dot(o_ref[...].astype(BF16), wao_ref[...])
    yc = _dot(ac_ref[...].astype(BF16), wco_ref[...])
    mix = ga_ref[...].astype(F32) * ya + gb_ref[...].astype(F32) * yc
    x1 = x_ref[...] + _dot(mix.astype(BF16), wout_ref[...])
    ms = jnp.mean(x1 * x1, axis=-1, keepdims=True)
    h2 = (x1 * lax.rsqrt(ms + EPS) * g2_ref[...]).astype(BF16)
    d_ff = wup_ref.shape[1]
    acc = x1
    for c in range(d_ff // ff_chunk):
        a = jnp.maximum(_dot(h2, wup_ref[:, c * ff_chunk:(c + 1) * ff_chunk]), 0.0)
        acc = acc + _dot((a * a).astype(BF16), wdn_ref[c * ff_chunk:(c + 1) * ff_chunk, :])
    y_ref[...] = acc


def _merge_mlp(x2d, o2d, ac2d, ga, gb, p, *, tm):
    n, d_model = x2d.shape
    d_ff = p["w_up"].shape[1]
    ch = ac2d.shape[1]
    row = lambda c: pl.BlockSpec((tm, c), lambda i: (i, 0))
    return pl.pallas_call(
        functools.partial(_merge_mlp_body, ff_chunk=512),
        grid=(n // tm,),
        in_specs=[row(d_model), row(ATTN_WIDTH), row(ch), row(d_model), row(d_model),
                  _const_spec((ATTN_WIDTH, d_model)), _const_spec((ch, d_model)),
                  _const_spec((d_model, d_model)), _const_spec((1, d_model)),
                  _const_spec((d_model, d_ff)), _const_spec((d_ff, d_model))],
        out_specs=row(d_model),
        out_shape=jax.ShapeDtypeStruct((n, d_model), F32),
        compiler_params=pltpu.CompilerParams(dimension_semantics=("arbitrary",),
                                             vmem_limit_bytes=VMEM_LIMIT_BYTES),
        name="merge_mlp",
    )(x2d, o2d, ac2d, ga, gb, p["w_attn_out"], p["w_conv_out"], p["w_out"], p["g2"], p["w_up"], p["w_down"])


def _prep_layer_params(norm1_g, w_in, b_forget, b_gate, q_norm_g, k_norm_g, w_attn_out, conv_dw_w,
                       conv_dw_b, conv_ln_g, conv_ln_b, w_conv_out, w_out, norm2_g, w_up, w_down):
    A = ATTN_WIDTH
    d_model = w_in.shape[0]
    ch = conv_dw_w.shape[1]
    f0 = 3 * A
    a0 = f0 + N_HEADS
    w_bf = w_in.astype(BF16)
    w_f = w_bf[:, f0:a0]
    w_cat = jnp.concatenate([w_bf[:, 0:f0], w_bf[:, a0:]], axis=1)
    hm = (lax.broadcasted_iota(jnp.int32, (A, A), 0) // HEAD_DIM
          == lax.broadcasted_iota(jnp.int32, (A, A), 1) // HEAD_DIM).astype(BF16) * (1.0 / HEAD_DIM)
    kg = jnp.tile(k_norm_g, N_HEADS)
    return dict(
        g1=norm1_g.reshape(1, d_model), w_cat=w_cat, w_f=w_f, w_f_t=w_f.T, w_kv_t=w_bf[:, A:f0].T,
        b_f=b_forget.reshape(1, N_HEADS), b_f_t=b_forget.reshape(N_HEADS, 1),
        b_gate=b_gate.reshape(1, 2 * d_model),
        qg=jnp.tile(q_norm_g, N_HEADS).reshape(1, A) * (HEAD_DIM ** -0.5),
        qg_log2=jnp.tile(q_norm_g, N_HEADS).reshape(1, A) * (HEAD_DIM ** -0.5 * LOG2E),
        kg=kg.reshape(1, A), kg_t=kg.reshape(A, 1),
        head_mean=hm.astype(BF16),
        conv_w=conv_dw_w, conv_b=conv_dw_b.reshape(1, ch), ln_g=conv_ln_g.reshape(1, ch),
        ln_b=conv_ln_b.reshape(1, ch),
        w_attn_out=w_attn_out.astype(BF16), w_conv_out=w_conv_out.astype(BF16),
        w_out=w_out.astype(BF16), g2=norm2_g.reshape(1, d_model),
        w_up=w_up.astype(BF16), w_down=w_down.astype(BF16))


def _pick_tile(n, pref):
    t = min(n, pref)
    while n % t:
        t //= 2
    return t


def _layer(xp, xs, cache_k, cache_v, cache_lf, state_conv, page_table, p):
    batch, seq_len, d_model = xp.shape
    dec_batch, dec_seq, _ = xs.shape
    n_pool, page, _, _ = cache_k.shape
    hist = CONV_WIDTH - 1

    n_p = batch * seq_len
    tm = _pick_tile(seq_len, 512)
    q, kt, vt, lft, u, ga, gb, ct = _in_proj(xp.reshape(n_p, d_model), p, tm=_pick_tile(seq_len, 1024),
                                             seq_len=seq_len, feature_major=True, q_dtype=BF16)
    o = _attn_prompt(q, kt, vt, ct, tq=_pick_tile(seq_len, 512))
    ac = _conv_prompt(u, p, batch=batch, seq_len=seq_len, tc=tm)
    yp = _merge_mlp(xp.reshape(n_p, d_model), o, ac, ga, gb, p, tm=tm).reshape(batch, seq_len, d_model)
    to_tok = lambda a: jnp.transpose(a.reshape(batch, N_HEADS, HEAD_DIM, seq_len), (0, 3, 1, 2))
    kp, vp = to_tok(kt), to_tok(vt)
    lp = jnp.transpose(lft, (0, 2, 1))
    cp = u.reshape(batch, seq_len, -1)[:, seq_len - hist:, :]

    n_s = dec_batch * dec_seq
    tms = _pick_tile(n_s, 512)
    qs, ks, vs, lfs, us, gas, gbs = _in_proj(xs.reshape(n_s, d_model), p, tm=tms, seq_len=dec_seq,
                                             feature_major=False, q_dtype=F32)
    r3 = lambda a: a.reshape(dec_batch, dec_seq, a.shape[-1])
    fm_cache = lambda c: jnp.transpose(c, (0, 2, 3, 1)).reshape(n_pool, ATTN_WIDTH, page)
    os_ = _attn_sample(r3(qs), r3(ks), r3(vs), r3(lfs), fm_cache(cache_k), fm_cache(cache_v),
                       jnp.transpose(cache_lf, (0, 2, 1)), page_table,
                       n_pg=_pick_tile(page_table.shape[1], 16))
    acs, new_state = _conv_sample(state_conv, r3(us), p, seqs_per_step=_pick_tile(dec_batch, 8))
    ys = _merge_mlp(xs.reshape(n_s, d_model), os_.reshape(n_s, ATTN_WIDTH), acs.reshape(n_s, -1),
                    gas, gbs, p, tm=tms).reshape(dec_batch, dec_seq, d_model)
    k_s = ks.reshape(dec_batch, dec_seq, N_HEADS, HEAD_DIM)
    v_s = vs.reshape(dec_batch, dec_seq, N_HEADS, HEAD_DIM)
    l_s = lfs.reshape(dec_batch, dec_seq, N_HEADS)
    return yp, ys, kp, vp, lp, cp, k_s, v_s, l_s, new_state


def kernel(x_prompt, x_sample, cache_k, cache_v, cache_logf, state_conv, page_table, norm1_g, w_in, b_forget, b_gate, q_norm_g, k_norm_g, w_attn_out, conv_dw_w, conv_dw_b, conv_ln_g, conv_ln_b, w_conv_out, w_out, norm2_g, w_up, w_down):
    depth = w_in.shape[0]
    xp, xs = x_prompt, x_sample
    outs = [[] for _ in range(8)]
    for l in range(depth):
        p = _prep_layer_params(norm1_g[l], w_in[l], b_forget[l], b_gate[l], q_norm_g[l], k_norm_g[l],
                               w_attn_out[l], conv_dw_w[l], conv_dw_b[l], conv_ln_g[l], conv_ln_b[l],
                               w_conv_out[l], w_out[l], norm2_g[l], w_up[l], w_down[l])
        xp, xs, *rest = _layer(xp, xs, cache_k[l], cache_v[l], cache_logf[l], state_conv[l], page_table, p)
        for acc, r in zip(outs, rest):
            acc.append(r)
    return (xp, xs) + tuple(jnp.stack(o) for o in outs)
```

```python
import functools

import jax
import jax.numpy as jnp
from jax import lax
from jax.experimental import pallas as pl
from jax.experimental.pallas import tpu as pltpu

N_HEADS = 8
HEAD_DIM = 64
ATTN_WIDTH = N_HEADS * HEAD_DIM
CONV_WIDTH = 31
EPS = 1e-6
NEG_BIG = -1e30
LOG2E = 1.4426950408889634

LANES = 128
SUBLANES = 8
HEADS_PER_VREG = LANES // HEAD_DIM
IN_PROJ_SUB_ROWS = 512
VMEM_LIMIT_BYTES = 56 * 1024 * 1024

F32 = jnp.float32
BF16 = jnp.bfloat16

NT_DIMS = (((1,), (1,)), ((), ()))


def _const_spec(shape):
    nd = len(shape)
    return pl.BlockSpec(shape, lambda *_: (0,) * nd, pipeline_mode=pl.Buffered(1))


def _split3(x):
    hi = x.astype(BF16)
    r = x - hi.astype(F32)
    mid = r.astype(BF16)
    lo = (r - mid.astype(F32)).astype(BF16)
    return hi, mid, lo


def _log_sigmoid(x):
    return jnp.minimum(x, 0.0) - jnp.log1p(jnp.exp(-jnp.abs(x)))


def _dot(a, b):
    return jnp.dot(a, b, preferred_element_type=F32)


def _dot_nt(a, b):
    return lax.dot_general(a, b, NT_DIMS, preferred_element_type=F32)


def _upper_incl(n):
    return (lax.broadcasted_iota(jnp.int32, (n, n), 0) <= lax.broadcasted_iota(jnp.int32, (n, n), 1)).astype(BF16)


def _in_proj_body(x_ref, g1_ref, w_ref, bg_ref, qg_ref, hm_ref, *rest, tiles_per_seq, feature_major, sub_rows):
    if feature_major:
        (wkvt_ref, kgt_ref, wft_ref, bft_ref, tri_ref,
         q_ref, kt_ref, vt_ref, lft_ref, u_ref, ga_ref, gb_ref, ct_ref, carry_ref) = rest

        @pl.when(pl.program_id(0) % tiles_per_seq == 0)
        def _():
            carry_ref[...] = jnp.zeros_like(carry_ref)

        carry = carry_ref[:, 0:1]
    else:
        wf_ref, bf_ref, kg_ref, q_ref, k_ref, v_ref, lf_ref, u_ref, ga_ref, gb_ref = rest

    A = ATTN_WIDTH
    tm, d_model = x_ref.shape
    hm = hm_ref[...]

    def head_rms(z, g):
        msq = _dot((z * z).astype(BF16), hm)
        return z * lax.rsqrt(msq + EPS) * g

    for r0 in range(0, tm, sub_rows):
        rs = slice(r0, r0 + sub_rows)
        x = x_ref[rs, :]
        ms = jnp.mean(x * x, axis=-1, keepdims=True)
        h = (x * lax.rsqrt(ms + EPS) * g1_ref[...]).astype(BF16)

        zq = _dot(h, w_ref[:, 0:A])
        q_ref[rs, :] = head_rms(zq, qg_ref[...]).astype(q_ref.dtype)

        if feature_major:
            zkt = _dot_nt(wkvt_ref[0:A, :], h)
            for hd in range(N_HEADS):
                rows = slice(hd * HEAD_DIM, (hd + 1) * HEAD_DIM)
                zh = zkt[rows, :]
                msq = jnp.mean(zh * zh, axis=0, keepdims=True)
                kt_ref[rows, rs] = zh * lax.rsqrt(msq + EPS) * kgt_ref[rows, :]
            vt_ref[:, rs] = _dot_nt(wkvt_ref[A:2 * A, :], h)

            lft = _log_sigmoid(_dot_nt(wft_ref[...], h) + bft_ref[...])
            lft_ref[:, rs] = lft
            tri = tri_ref[...]
            c = carry
            for part in _split3(lft):
                c = c + _dot(part, tri)
            ct_ref[:, rs] = c
            carry = c[:, sub_rows - 1:sub_rows]
        else:
            zk = _dot(h, w_ref[:, A:2 * A])
            k_ref[rs, :] = head_rms(zk, kg_ref[...])
            v_ref[rs, :] = _dot(h, w_ref[:, 2 * A:3 * A])
            lf_ref[rs, :] = _log_sigmoid(_dot(h, wf_ref[...]) + bf_ref[...])

        za = _dot(h, w_ref[:, 3 * A:4 * A])
        zu = _dot(h, w_ref[:, 4 * A:5 * A])
        u_ref[rs, :] = za * jax.nn.sigmoid(zu)

        o = 5 * A
        zga = _dot(h, w_ref[:, o:o + d_model])
        ga_ref[rs, :] = jax.nn.sigmoid(zga + bg_ref[:, 0:d_model]).astype(ga_ref.dtype)
        zgb = _dot(h, w_ref[:, o + d_model:o + 2 * d_model])
        gb_ref[rs, :] = jax.nn.sigmoid(zgb + bg_ref[:, d_model:2 * d_model]).astype(gb_ref.dtype)

    if feature_major:
        carry_ref[...] = jnp.broadcast_to(carry, carry_ref.shape)


def _in_proj(x2d, p, *, tm, seq_len, feature_major, q_dtype):
    n, d_model = x2d.shape
    assert n % tm == 0
    n_tiles = n // tm
    sub_rows = _pick_tile(tm, IN_PROJ_SUB_ROWS)
    A = ATTN_WIDTH
    wcols = p["w_cat"].shape[1]

    row = lambda c: pl.BlockSpec((tm, c), lambda i: (i, 0))
    in_specs = [row(d_model), _const_spec((1, d_model)), _const_spec((d_model, wcols)),
                _const_spec((1, 2 * d_model)), _const_spec((1, A)), _const_spec((A, A))]
    args = [x2d, p["g1"], p["w_cat"], p["b_gate"], p["qg_log2"] if feature_major else p["qg"], p["head_mean"]]
    tok = lambda c, dt: jax.ShapeDtypeStruct((n, c), dt)
    tail_shapes = [tok(A, F32), tok(d_model, BF16), tok(d_model, BF16)]
    tail_specs = [row(A), row(d_model), row(d_model)]
    scratch = []
    tiles_per_seq = 1
    if feature_major:
        assert seq_len % tm == 0
        tiles_per_seq = seq_len // tm
        batch = n // seq_len
        in_specs += [_const_spec((2 * A, d_model)), _const_spec((A, 1)), _const_spec((N_HEADS, d_model)),
                     _const_spec((N_HEADS, 1)), _const_spec((sub_rows, sub_rows))]
        args += [p["w_kv_t"], p["kg_t"], p["w_f_t"], p["b_f_t"], _upper_incl(sub_rows)]
        fm = lambda r: pl.BlockSpec((None, r, tm), lambda i: (i // tiles_per_seq, 0, i % tiles_per_seq))
        fms = lambda r: jax.ShapeDtypeStruct((batch, r, seq_len), F32)
        out_shape = [tok(A, q_dtype), fms(A), fms(A), fms(N_HEADS)] + tail_shapes
        out_shape.append(jax.ShapeDtypeStruct((batch * N_HEADS, seq_len), F32))
        out_specs = [row(A), fm(A), fm(A), fm(N_HEADS)] + tail_specs
        out_specs.append(pl.BlockSpec((N_HEADS, tm), lambda i: (i // tiles_per_seq, i % tiles_per_seq)))
        scratch.append(pltpu.VMEM((N_HEADS, LANES), F32))
    else:
        in_specs += [_const_spec((d_model, N_HEADS)), _const_spec((1, N_HEADS)), _const_spec((1, A))]
        args += [p["w_f"], p["b_f"], p["kg"]]
        out_shape = [tok(A, q_dtype), tok(A, F32), tok(A, F32), tok(N_HEADS, F32)] + tail_shapes
        out_specs = [row(A), row(A), row(A), row(N_HEADS)] + tail_specs

    return pl.pallas_call(
        functools.partial(_in_proj_body, tiles_per_seq=tiles_per_seq, feature_major=feature_major,
                          sub_rows=sub_rows),
        grid=(n_tiles,),
        in_specs=in_specs,
        out_specs=out_specs,
        out_shape=out_shape,
        scratch_shapes=scratch,
        compiler_params=pltpu.CompilerParams(dimension_semantics=("arbitrary",),
                                             vmem_limit_bytes=VMEM_LIMIT_BYTES),
        name="in_proj_fm" if feature_major else "in_proj",
    )(*args)


def _attn_prompt_body(q_ref, kt_ref, vt_ref, c_ref, o_ref, kb_ref, vb_ref, *, tq):
    hp = pl.program_id(1)
    seq = q_ref.shape[0]
    vb_ref[...] = vt_ref[...].astype(BF16)
    lane = lax.broadcasted_iota(jnp.int32, (1, LANES), 1)
    first_head = lane < HEAD_DIM
    causal = (lax.broadcasted_iota(jnp.int32, (tq, tq), 1)
              <= lax.broadcasted_iota(jnp.int32, (tq, tq), 0))
    kt = kt_ref[...]
    c_all = c_ref[...] * LOG2E
    sub = lax.broadcasted_iota(jnp.int32, c_all.shape, 0)
    q_fill = []
    for hh in range(HEADS_PER_VREG):
        c_row = jnp.sum(jnp.where(sub == HEADS_PER_VREG * hp + hh, c_all, 0.0), axis=0, keepdims=True)
        hi, mid, lo = (part.astype(F32) for part in _split3(-c_row))
        base = (1 - hh) * HEAD_DIM
        blk = kt[base:base + SUBLANES, :]
        blk = jnp.where(sub == 0, hi, jnp.where(sub == 1, mid, jnp.where(sub == 2, lo, blk)))
        pieces = [kt[0:base, :]] * (base > 0) + [blk, kt[base + SUBLANES:, :]]
        kb_ref[hh] = jnp.concatenate(pieces, axis=0).astype(BF16)
        q_fill.append(jnp.where((lane >= base) & (lane < base + 3), 1.0, 0.0).astype(BF16))

    for i in range(seq // tq):
        q0 = i * tq
        q = q_ref[q0:q0 + tq, :]
        outs = []
        for hh in range(HEADS_PER_VREG):
            own = first_head if hh == 0 else jnp.logical_not(first_head)
            qm = jnp.where(own, q, q_fill[hh])
            m = jnp.full((tq, 1), NEG_BIG, F32)
            l = jnp.zeros((tq, 1), F32)
            acc = jnp.zeros((tq, LANES), F32)
            for j in range(i + 1):
                k0 = j * tq
                s = _dot(qm, kb_ref[hh, :, k0:k0 + tq])
                if j == i:
                    s = jnp.where(causal, s, NEG_BIG)
                m_new = jnp.maximum(m, jnp.max(s, axis=1, keepdims=True))
                alpha = jnp.exp2(m - m_new)
                pr = jnp.exp2(s - m_new)
                l = alpha * l + jnp.sum(pr, axis=1, keepdims=True)
                acc = alpha * acc + _dot_nt(pr.astype(BF16), vb_ref[:, k0:k0 + tq])
                m = m_new
            outs.append(acc * (1.0 / l))
        o_ref[q0:q0 + tq, :] = jnp.where(first_head, outs[0], outs[1]).astype(o_ref.dtype)


def _attn_prompt(q2d, kt, vt, ct, *, tq):
    batch, _, seq_len = kt.shape
    n = q2d.shape[0]
    pairs = N_HEADS // HEADS_PER_VREG
    fm = pl.BlockSpec((None, LANES, seq_len), lambda b, hp: (b, hp, 0))
    tok = pl.BlockSpec((seq_len, LANES), lambda b, hp: (b, hp))
    return pl.pallas_call(
        functools.partial(_attn_prompt_body, tq=tq),
        grid=(batch, pairs),
        in_specs=[tok, fm, fm, pl.BlockSpec((N_HEADS, seq_len), lambda b, hp: (b, 0))],
        out_specs=tok,
        out_shape=jax.ShapeDtypeStruct((n, ATTN_WIDTH), BF16),
        scratch_shapes=[pltpu.VMEM((HEADS_PER_VREG, LANES, seq_len), BF16), pltpu.VMEM((LANES, seq_len), BF16)],
        compiler_params=pltpu.CompilerParams(dimension_semantics=("arbitrary", "arbitrary"),
                                             vmem_limit_bytes=VMEM_LIMIT_BYTES),
        name="attn_prompt",
    )(q2d, kt, vt, ct)


def _attn_sample_body(pt_ref, q_ref, ks_ref, vs_ref, lfs_ref, ckt_hbm, cvt_hbm, clt_hbm,
                      lincl_ref, sel_ref, uincl_ref, o_ref,
                      qbd_ref, m_ref, l_ref, acc_ref, tot_ref, lfpad_ref, new_ref,
                      kbuf, vbuf, lbuf, sem, *, n_pg, n_pages, dec_seq):
    g = pl.program_id(1)
    steps = pl.num_programs(1)
    last_step = pl.num_programs(0) * steps - 1
    step = pl.program_id(0) * steps + g
    slot = step % 2

    def page_copies(of_step, into_slot, lookup=True):
        base = (of_step // steps) * n_pages + n_pages - (of_step % steps + 1) * n_pg
        copies = []
        for i in range(n_pg):
            idx = pt_ref[base + i] if lookup else 0
            copies.append(pltpu.make_async_copy(ckt_hbm.at[idx], kbuf.at[into_slot, i], sem.at[into_slot, 0]))
            copies.append(pltpu.make_async_copy(cvt_hbm.at[idx], vbuf.at[into_slot, i], sem.at[into_slot, 1]))
            copies.append(pltpu.make_async_copy(clt_hbm.at[idx], lbuf.at[into_slot, i], sem.at[into_slot, 2]))
        return copies

    @pl.when(step == 0)
    def _():
        for cp in page_copies(0, 0):
            cp.start()

    rows = N_HEADS * dec_seq
    page = lfpad_ref.shape[0]
    row_head = lax.broadcasted_iota(jnp.int32, (rows, ATTN_WIDTH), 0) // dec_seq
    col_head = lax.broadcasted_iota(jnp.int32, (rows, ATTN_WIDTH), 1) // HEAD_DIM
    own_head = row_head == col_head

    def softmax_update(s, pv_fn):
        m = m_ref[...]
        m_new = jnp.maximum(m, jnp.max(s, axis=1, keepdims=True))
        alpha = jnp.exp(m - m_new)
        pr = jnp.exp(s - m_new)
        l_ref[...] = alpha * l_ref[...] + jnp.sum(pr, axis=1, keepdims=True)
        acc_ref[...] = alpha * acc_ref[...] + pv_fn(pr.astype(BF16))
        m_ref[...] = m_new

    @pl.when(g == 0)
    def _():
        q = q_ref[...]
        q_rows = jnp.concatenate([q] * N_HEADS, axis=0)
        qbd_ref[...] = jnp.where(own_head, q_rows, 0.0).astype(BF16)
        m_ref[...] = jnp.full(m_ref.shape, NEG_BIG, F32)
        l_ref[...] = jnp.zeros(l_ref.shape, F32)
        acc_ref[...] = jnp.zeros(acc_ref.shape, F32)
        tot_ref[...] = jnp.zeros(tot_ref.shape, F32)
        key = lax.broadcasted_iota(jnp.int32, (rows, page), 1)
        t = lax.broadcasted_iota(jnp.int32, (rows, page), 0) % dec_seq
        for slot, src in ((0, ks_ref), (1, vs_ref)):
            new_ref[slot] = jnp.zeros(new_ref.shape[1:], F32)
            new_ref[slot, 0:dec_seq, :] = src[...]
        k_new = new_ref[0].astype(BF16)
        v_new = new_ref[1].astype(BF16)
        lfpad_ref[...] = jnp.zeros(lfpad_ref.shape, F32)
        lfpad_ref[0:dec_seq, 0:N_HEADS] = lfs_ref[...]
        lincl = lincl_ref[...]
        pre = jnp.zeros((page, LANES), F32)
        for part in _split3(lfpad_ref[...]):
            pre = pre + _dot(lincl, part)
        sel = sel_ref[...]
        pre_rows = jnp.zeros((rows, page), F32)
        for part in _split3(pre):
            pre_rows = pre_rows + _dot_nt(sel, part)
        tot_new = pre_rows[:, page - 1:page]
        s = _dot_nt(qbd_ref[...], k_new) + (tot_new - pre_rows)
        softmax_update(jnp.where(key <= t, s, NEG_BIG), lambda pr: _dot(pr, v_new))
        tot_ref[...] = tot_new

    for cp in page_copies(step, slot, lookup=False):
        cp.wait()
    for cp in page_copies(jnp.minimum(step + 1, last_step), 1 - slot):
        cp.start()
    kp = [kbuf.at[slot, i] for i in range(n_pg)]
    vp = [vbuf.at[slot, i] for i in range(n_pg)]
    lp = [lbuf.at[slot, i] for i in range(n_pg)]

    lft = jnp.concatenate([lp[i][...] for i in range(n_pg)], axis=0)
    uincl = uincl_ref[...]
    pre_all = jnp.zeros(lft.shape, F32)
    for part in _split3(lft):
        pre_all = pre_all + _dot(part, uincl)
    qbd = qbd_ref[...]
    run = tot_ref[...]
    scores = [None] * n_pg
    for i in reversed(range(n_pg)):
        pre_rows = jnp.concatenate(
            [jnp.broadcast_to(pre_all[i * N_HEADS + hd:i * N_HEADS + hd + 1, :], (dec_seq, page))
             for hd in range(N_HEADS)], axis=0)
        run = run + pre_rows[:, page - 1:page]
        scores[i] = _dot(qbd, kp[i][...].astype(BF16)) + (run - pre_rows)
    tot_ref[...] = run

    def pv_pages(pr):
        out = _dot_nt(pr[:, 0:page], vp[0][...].astype(BF16))
        for i in range(1, n_pg):
            out = out + _dot_nt(pr[:, i * page:(i + 1) * page], vp[i][...].astype(BF16))
        return out

    softmax_update(jnp.concatenate(scores, axis=1), pv_pages)

    @pl.when(g == pl.num_programs(1) - 1)
    def _():
        o = jnp.where(own_head, acc_ref[...] * (1.0 / l_ref[...]), 0.0)
        out = o[0:dec_seq, :]
        for hd in range(1, N_HEADS):
            out = out + o[hd * dec_seq:(hd + 1) * dec_seq, :]
        o_ref[...] = out

    @pl.when(step == last_step)
    def _():
        for cp in page_copies(last_step, 1 - slot, lookup=False):
            cp.wait()


def _attn_sample(q3, k3, v3, lf3, cache_kt, cache_vt, cache_lft, page_table, *, n_pg):
    dec_batch, dec_seq, _ = q3.shape
    n_pool, _, page = cache_kt.shape
    n_pages = page_table.shape[1]
    assert n_pages % n_pg == 0 and dec_seq == SUBLANES and page == LANES
    steps = n_pages // n_pg
    rows = N_HEADS * dec_seq

    uincl = _upper_incl(page)
    sel = (lax.broadcasted_iota(jnp.int32, (rows, LANES), 0) // dec_seq
           == lax.broadcasted_iota(jnp.int32, (rows, LANES), 1)).astype(BF16)

    per_b = lambda c: pl.BlockSpec((None, dec_seq, c), lambda b, g, pt: (b, 0, 0))
    const = lambda shape: pl.BlockSpec(shape, lambda b, g, pt: (0, 0))
    in_hbm = pl.BlockSpec(memory_space=pl.ANY)
    in_specs = [per_b(ATTN_WIDTH), per_b(ATTN_WIDTH), per_b(ATTN_WIDTH), per_b(N_HEADS)]
    in_specs += [in_hbm, in_hbm, in_hbm]
    in_specs += [const((page, page)), const((rows, LANES)), const((page, page))]
    grid_spec = pltpu.PrefetchScalarGridSpec(
        num_scalar_prefetch=1,
        grid=(dec_batch, steps),
        in_specs=in_specs,
        out_specs=pl.BlockSpec((None, dec_seq, ATTN_WIDTH), lambda b, g, pt: (b, 0, 0)),
        scratch_shapes=[pltpu.VMEM((rows, ATTN_WIDTH), BF16), pltpu.VMEM((rows, 1), F32),
                        pltpu.VMEM((rows, 1), F32), pltpu.VMEM((rows, ATTN_WIDTH), F32),
                        pltpu.VMEM((rows, 1), F32), pltpu.VMEM((page, LANES), F32),
                        pltpu.VMEM((2, page, ATTN_WIDTH), F32),
                        pltpu.VMEM((2, n_pg, ATTN_WIDTH, page), F32), pltpu.VMEM((2, n_pg, ATTN_WIDTH, page), F32),
                        pltpu.VMEM((2, n_pg, N_HEADS, page), F32), pltpu.SemaphoreType.DMA((2, 3))],
    )
    args = [page_table.reshape(-1), q3, k3, v3, lf3, cache_kt, cache_vt, cache_lft, uincl.T, sel, uincl]
    return pl.pallas_call(
        functools.partial(_attn_sample_body, n_pg=n_pg, n_pages=n_pages, dec_seq=dec_seq),
        grid_spec=grid_spec,
        out_shape=jax.ShapeDtypeStruct((dec_batch, dec_seq, ATTN_WIDTH), F32),
        compiler_params=pltpu.CompilerParams(dimension_semantics=("arbitrary", "arbitrary"),
                                             vmem_limit_bytes=VMEM_LIMIT_BYTES),
        name="attn_sample",
    )(*args)


def _ln_swish(c, g, b):
    mu = jnp.mean(c, axis=-1, keepdims=True)
    d = c - mu
    var = jnp.mean(d * d, axis=-1, keepdims=True)
    y = d * lax.rsqrt(var + EPS) * g + b
    return y * jax.nn.sigmoid(y)


HIST_ROWS = 32
HIST_PAD = HIST_ROWS - (CONV_WIDTH - 1)
TAP_GROUPS = -(-(HIST_PAD + CONV_WIDTH) // SUBLANES)


def _conv_prompt_body(u_ref, w_ref, b_ref, g_ref, beta_ref, o_ref, ext_ref, *, rows_per_chunk):
    tc, ch = u_ref.shape
    R = rows_per_chunk

    @pl.when(pl.program_id(1) == 0)
    def _():
        ext_ref[0:HIST_ROWS, :] = jnp.zeros((HIST_ROWS, ch), F32)
        ext_ref[HIST_ROWS + tc:HIST_ROWS + tc + SUBLANES, :] = jnp.zeros((SUBLANES, ch), F32)

    ext_ref[HIST_ROWS:HIST_ROWS + tc, :] = u_ref[...]

    for r0 in range(0, tc, R):
        cols = []
        for cb in range(ch // LANES):
            lanes = slice(cb * LANES, (cb + 1) * LANES)
            y = jnp.zeros((R, LANES), F32) + b_ref[:, lanes]
            for r in range(SUBLANES):
                part = None
                for a in range(TAP_GROUPS):
                    k = SUBLANES * a + r - HIST_PAD
                    if 0 <= k < CONV_WIDTH:
                        term = ext_ref[r0 + SUBLANES * a:r0 + SUBLANES * a + R + SUBLANES, lanes] * w_ref[k:k + 1, lanes]
                        part = term if part is None else part + term
                y = y + part[r:r + R, :]
            cols.append(y)
        acc = jnp.concatenate(cols, axis=1)
        o_ref[r0:r0 + R, :] = _ln_swish(acc, g_ref[...], beta_ref[...]).astype(o_ref.dtype)
    ext_ref[0:HIST_ROWS, :] = ext_ref[tc:tc + HIST_ROWS, :]


def _conv_prompt(u2d, p, *, batch, seq_len, tc):
    n, ch = u2d.shape
    tiles = seq_len // tc
    return pl.pallas_call(
        functools.partial(_conv_prompt_body, rows_per_chunk=_pick_tile(tc, 128)),
        grid=(batch, tiles),
        in_specs=[pl.BlockSpec((tc, ch), lambda b, i: (b * tiles + i, 0)),
                  _const_spec((CONV_WIDTH, ch)), _const_spec((1, ch)), _const_spec((1, ch)),
                  _const_spec((1, ch))],
        out_specs=pl.BlockSpec((tc, ch), lambda b, i: (b * tiles + i, 0)),
        out_shape=jax.ShapeDtypeStruct((n, ch), BF16),
        scratch_shapes=[pltpu.VMEM((HIST_ROWS + tc + SUBLANES, ch), F32)],
        compiler_params=pltpu.CompilerParams(dimension_semantics=("arbitrary", "arbitrary"),
                                             vmem_limit_bytes=VMEM_LIMIT_BYTES),
        name="conv_prompt",
    )(u2d, p["conv_w"], p["conv_b"], p["ln_g"], p["ln_b"])


def _conv_sample_body(st_ref, u_ref, w_ref, b_ref, g_ref, beta_ref, o_ref, ns_ref, ext_ref):
    hist = CONV_WIDTH - 1
    dec_seq = u_ref.shape[1]

    def one_seq(sb, _):
        ext_ref[0:hist, :] = st_ref[sb]
        ext_ref[hist:hist + dec_seq, :] = u_ref[sb]
        acc = jnp.zeros((dec_seq, ext_ref.shape[1]), F32) + b_ref[...]
        for k in range(CONV_WIDTH):
            acc = acc + ext_ref[k:k + dec_seq, :] * w_ref[k:k + 1, :]
        o_ref[sb] = _ln_swish(acc, g_ref[...], beta_ref[...]).astype(o_ref.dtype)
        ns_ref[sb] = ext_ref[dec_seq:dec_seq + hist, :]
        return 0

    lax.fori_loop(0, u_ref.shape[0], one_seq, 0)


def _conv_sample(state, u3, p, *, seqs_per_step):
    dec_batch, dec_seq, ch = u3.shape
    hist = CONV_WIDTH - 1
    blk = lambda r: pl.BlockSpec((seqs_per_step, r, ch), lambda i: (i, 0, 0))
    return pl.pallas_call(
        _conv_sample_body,
        grid=(dec_batch // seqs_per_step,),
        in_specs=[blk(hist), blk(dec_seq), _const_spec((CONV_WIDTH, ch)), _const_spec((1, ch)),
                  _const_spec((1, ch)), _const_spec((1, ch))],
        out_specs=[blk(dec_seq), blk(hist)],
        out_shape=[jax.ShapeDtypeStruct((dec_batch, dec_seq, ch), F32),
                   jax.ShapeDtypeStruct((dec_batch, hist, ch), F32)],
        scratch_shapes=[pltpu.VMEM((hist + dec_seq + 2, ch), F32)],
        compiler_params=pltpu.CompilerParams(dimension_semantics=("arbitrary",),
                                             vmem_limit_bytes=VMEM_LIMIT_BYTES),
        name="conv_sample",
    )(state, u3, p["conv_w"], p["conv_b"], p["ln_g"], p["ln_b"])


def _merge_mlp_body(x_ref, o_ref, ac_ref, ga_ref, gb_ref, wao_ref, wco_ref, wout_ref, g2_ref,
                    wup_ref, wdn_ref, y_ref, *, ff_chunk):
    ya = _dot(o_ref[...].astype(BF16), wao_ref[...])
    yc = _dot(ac_ref[...].astype(BF16), wco_ref[...])
    mix = ga_ref[...].astype(F32) * ya + gb_ref[...].astype(F32) * yc
    x1 = x_ref[...] + _dot(mix.astype(BF16), wout_ref[...])
    ms = jnp.mean(x1 * x1, axis=-1, keepdims=True)
    h2 = (x1 * lax.rsqrt(ms + EPS) * g2_ref[...]).astype(BF16)
    d_ff = wup_ref.shape[1]
    acc = x1
    for c in range(d_ff // ff_chunk):
        a = jnp.maximum(_dot(h2, wup_ref[:, c * ff_chunk:(c + 1) * ff_chunk]), 0.0)
        acc = acc + _dot((a * a).astype(BF16), wdn_ref[c * ff_chunk:(c + 1) * ff_chunk, :])
    y_ref[...] = acc


def _merge_mlp(x2d, o2d, ac2d, ga, gb, p, *, tm):
    n, d_model = x2d.shape
    d_ff = p["w_up"].shape[1]
    ch = ac2d.shape[1]
    row = lambda c: pl.BlockSpec((tm, c), lambda i: (i, 0))
    return pl.pallas_call(
        functools.partial(_merge_mlp_body, ff_chunk=512),
        grid=(n // tm,),
        in_specs=[row(d_model), row(ATTN_WIDTH), row(ch), row(d_model), row(d_model),
                  _const_spec((ATTN_WIDTH, d_model)), _const_spec((ch, d_model)),
                  _const_spec((d_model, d_model)), _const_spec((1, d_model)),
                  _const_spec((d_model, d_ff)), _const_spec((d_ff, d_model))],
        out_specs=row(d_model),
        out_shape=jax.ShapeDtypeStruct((n, d_model), F32),
        compiler_params=pltpu.CompilerParams(dimension_semantics=("arbitrary",),
                                             vmem_limit_bytes=VMEM_LIMIT_BYTES),
        name="merge_mlp",
    )(x2d, o2d, ac2d, ga, gb, p["w_attn_out"], p["w_conv_out"], p["w_out"], p["g2"], p["w_up"], p["w_down"])


def _prep_layer_params(norm1_g, w_in, b_forget, b_gate, q_norm_g, k_norm_g, w_attn_out, conv_dw_w,
                       conv_dw_b, conv_ln_g, conv_ln_b, w_conv_out, w_out, norm2_g, w_up, w_down):
    A = ATTN_WIDTH
    d_model = w_in.shape[0]
    ch = conv_dw_w.shape[1]
    f0 = 3 * A
    a0 = f0 + N_HEADS
    w_bf = w_in.astype(BF16)
    w_f = w_bf[:, f0:a0]
    w_cat = jnp.concatenate([w_bf[:, 0:f0], w_bf[:, a0:]], axis=1)
    hm = (lax.broadcasted_iota(jnp.int32, (A, A), 0) // HEAD_DIM
          == lax.broadcasted_iota(jnp.int32, (A, A), 1) // HEAD_DIM).astype(BF16) * (1.0 / HEAD_DIM)
    kg = jnp.tile(k_norm_g, N_HEADS)
    return dict(
        g1=norm1_g.reshape(1, d_model), w_cat=w_cat, w_f=w_f, w_f_t=w_f.T, w_kv_t=w_bf[:, A:f0].T,
        b_f=b_forget.reshape(1, N_HEADS), b_f_t=b_forget.reshape(N_HEADS, 1),
        b_gate=b_gate.reshape(1, 2 * d_model),
        qg=jnp.tile(q_norm_g, N_HEADS).reshape(1, A) * (HEAD_DIM ** -0.5),
        qg_log2=jnp.tile(q_norm_g, N_HEADS).reshape(1, A) * (HEAD_DIM ** -0.5 * LOG2E),
        kg=kg.reshape(1, A), kg_t=kg.reshape(A, 1),
        head_mean=hm.astype(BF16),
        conv_w=conv_dw_w, conv_b=conv_dw_b.reshape(1, ch), ln_g=conv_ln_g.reshape(1, ch),
        ln_b=conv_ln_b.reshape(1, ch),
        w_attn_out=w_attn_out.astype(BF16), w_conv_out=w_conv_out.astype(BF16),
        w_out=w_out.astype(BF16), g2=norm2_g.reshape(1, d_model),
        w_up=w_up.astype(BF16), w_down=w_down.astype(BF16))


def _pick_tile(n, pref):
    t = min(n, pref)
    while n % t:
        t //= 2
    return t


def _layer(xp, xs, cache_k, cache_v, cache_lf, state_conv, page_table, p):
    batch, seq_len, d_model = xp.shape
    dec_batch, dec_seq, _ = xs.shape
    n_pool, page, _, _ = cache_k.shape
    hist = CONV_WIDTH - 1

    n_p = batch * seq_len
    tm = _pick_tile(seq_len, 512)
    q, kt, vt, lft, u, ga, gb, ct = _in_proj(xp.reshape(n_p, d_model), p, tm=_pick_tile(seq_len, 1024),
                                             seq_len=seq_len, feature_major=True, q_dtype=BF16)
    o = _attn_prompt(q, kt, vt, ct, tq=_pick_tile(seq_len, 512))
    ac = _conv_prompt(u, p, batch=batch, seq_len=seq_len, tc=tm)
    yp = _merge_mlp(xp.reshape(n_p, d_model), o, ac, ga, gb, p, tm=tm).reshape(batch, seq_len, d_model)
    to_tok = lambda a: jnp.transpose(a.reshape(batch, N_HEADS, HEAD_DIM, seq_len), (0, 3, 1, 2))
    kp, vp = to_tok(kt), to_tok(vt)
    lp = jnp.transpose(lft, (0, 2, 1))
    cp = u.reshape(batch, seq_len, -1)[:, seq_len - hist:, :]

    n_s = dec_batch * dec_seq
    tms = _pick_tile(n_s, 512)
    qs, ks, vs, lfs, us, gas, gbs = _in_proj(xs.reshape(n_s, d_model), p, tm=tms, seq_len=dec_seq,
                                             feature_major=False, q_dtype=F32)
    r3 = lambda a: a.reshape(dec_batch, dec_seq, a.shape[-1])
    fm_cache = lambda c: jnp.transpose(c, (0, 2, 3, 1)).reshape(n_pool, ATTN_WIDTH, page)
    os_ = _attn_sample(r3(qs), r3(ks), r3(vs), r3(lfs), fm_cache(cache_k), fm_cache(cache_v),
                       jnp.transpose(cache_lf, (0, 2, 1)), page_table,
                       n_pg=_pick_tile(page_table.shape[1], 16))
    acs, new_state = _conv_sample(state_conv, r3(us), p, seqs_per_step=_pick_tile(dec_batch, 8))
    ys = _merge_mlp(xs.reshape(n_s, d_model), os_.reshape(n_s, ATTN_WIDTH), acs.reshape(n_s, -1),
                    gas, gbs, p, tm=tms).reshape(dec_batch, dec_seq, d_model)
    k_s = ks.reshape(dec_batch, dec_seq, N_HEADS, HEAD_DIM)
    v_s = vs.reshape(dec_batch, dec_seq, N_HEADS, HEAD_DIM)
    l_s = lfs.reshape(dec_batch, dec_seq, N_HEADS)
    return yp, ys, kp, vp, lp, cp, k_s, v_s, l_s, new_state


def kernel(x_prompt, x_sample, cache_k, cache_v, cache_logf, state_conv, page_table, norm1_g, w_in, b_forget, b_gate, q_norm_g, k_norm_g, w_attn_out, conv_dw_w, conv_dw_b, conv_ln_g, conv_ln_b, w_conv_out, w_out, norm2_g, w_up, w_down):
    depth = w_in.shape[0]
    xp, xs = x_prompt, x_sample
    outs = [[] for _ in range(8)]
    for l in range(depth):
        p = _prep_layer_params(norm1_g[l], w_in[l], b_forget[l], b_gate[l], q_norm_g[l], k_norm_g[l],
                               w_attn_out[l], conv_dw_w[l], conv_dw_b[l], conv_ln_g[l], conv_ln_b[l],
                               w_conv_out[l], w_out[l], norm2_g[l], w_up[l], w_down[l])
        xp, xs, *rest = _layer(xp, xs, cache_k[l], cache_v[l], cache_logf[l], state_conv[l], page_table, p)
        for acc, r in zip(outs, rest):
            acc.append(r)
    return (xp, xs) + tuple(jnp.stack(o) for o in outs)
```

```python
import functools

import jax
import jax.numpy as jnp
from jax import lax
from jax.experimental import pallas as pl
from jax.experimental.pallas import tpu as pltpu

N_HEADS = 8
HEAD_DIM = 64
ATTN_WIDTH = N_HEADS * HEAD_DIM
CONV_WIDTH = 31
EPS = 1e-6
NEG_BIG = -1e30
LOG2E = 1.4426950408889634

LANES = 128
SUBLANES = 8
HEADS_PER_VREG = LANES // HEAD_DIM
IN_PROJ_SUB_ROWS = 512
VMEM_LIMIT_BYTES = 56 * 1024 * 1024

F32 = jnp.float32
BF16 = jnp.bfloat16

NT_DIMS = (((1,), (1,)), ((), ()))


def _const_spec(shape):
    nd = len(shape)
    return pl.BlockSpec(shape, lambda *_: (0,) * nd, pipeline_mode=pl.Buffered(1))


def _split3(x):
    hi = x.astype(BF16)
    r = x - hi.astype(F32)
    mid = r.astype(BF16)
    lo = (r - mid.astype(F32)).astype(BF16)
    return hi, mid, lo


def _log_sigmoid(x):
    return jnp.minimum(x, 0.0) - jnp.log1p(jnp.exp(-jnp.abs(x)))


def _dot(a, b):
    return jnp.dot(a, b, preferred_element_type=F32)


def _dot_nt(a, b):
    return lax.dot_general(a, b, NT_DIMS, preferred_element_type=F32)


def _upper_incl(n):
    return (lax.broadcasted_iota(jnp.int32, (n, n), 0) <= lax.broadcasted_iota(jnp.int32, (n, n), 1)).astype(BF16)


def _in_proj_body(x_ref, g1_ref, w_ref, bg_ref, qg_ref, hm_ref, *rest, tiles_per_seq, feature_major, sub_rows):
    if feature_major:
        (wkvt_ref, kgt_ref, wft_ref, bft_ref, tri_ref,
         q_ref, kt_ref, vt_ref, lft_ref, u_ref, ga_ref, gb_ref, ct_ref, carry_ref) = rest

        @pl.when(pl.program_id(0) % tiles_per_seq == 0)
        def _():
            carry_ref[...] = jnp.zeros_like(carry_ref)

        carry = carry_ref[:, 0:1]
    else:
        wf_ref, bf_ref, kg_ref, q_ref, k_ref, v_ref, lf_ref, u_ref, ga_ref, gb_ref = rest

    A = ATTN_WIDTH
    tm, d_model = x_ref.shape
    hm = hm_ref[...]

    def head_rms(z, g):
        msq = _dot((z * z).astype(BF16), hm)
        return z * lax.rsqrt(msq + EPS) * g

    for r0 in range(0, tm, sub_rows):
        rs = slice(r0, r0 + sub_rows)
        x = x_ref[rs, :]
        ms = jnp.mean(x * x, axis=-1, keepdims=True)
        h = (x * lax.rsqrt(ms + EPS) * g1_ref[...]).astype(BF16)

        zq = _dot(h, w_ref[:, 0:A])
        q_ref[rs, :] = head_rms(zq, qg_ref[...]).astype(q_ref.dtype)

        if feature_major:
            zkt = _dot_nt(wkvt_ref[0:A, :], h)
            for hd in range(N_HEADS):
                rows = slice(hd * HEAD_DIM, (hd + 1) * HEAD_DIM)
                zh = zkt[rows, :]
                msq = jnp.mean(zh * zh, axis=0, keepdims=True)
                kt_ref[rows, rs] = zh * lax.rsqrt(msq + EPS) * kgt_ref[rows, :]
            vt_ref[:, rs] = _dot_nt(wkvt_ref[A:2 * A, :], h)

            lft = _log_sigmoid(_dot_nt(wft_ref[...], h) + bft_ref[...])
            lft_ref[:, rs] = lft
            tri = tri_ref[...]
            c = carry
            for part in _split3(lft):
                c = c + _dot(part, tri)
            ct_ref[:, rs] = c
            carry = c[:, sub_rows - 1:sub_rows]
        else:
            zk = _dot(h, w_ref[:, A:2 * A])
            k_ref[rs, :] = head_rms(zk, kg_ref[...])
            v_ref[rs, :] = _dot(h, w_ref[:, 2 * A:3 * A])
            lf_ref[rs, :] = _log_sigmoid(_dot(h, wf_ref[...]) + bf_ref[...])

        za = _dot(h, w_ref[:, 3 * A:4 * A])
        zu = _dot(h, w_ref[:, 4 * A:5 * A])
        u_ref[rs, :] = za * jax.nn.sigmoid(zu)

        o = 5 * A
        zga = _dot(h, w_ref[:, o:o + d_model])
        ga_ref[rs, :] = jax.nn.sigmoid(zga + bg_ref[:, 0:d_model]).astype(ga_ref.dtype)
        zgb = _dot(h, w_ref[:, o + d_model:o + 2 * d_model])
        gb_ref[rs, :] = jax.nn.sigmoid(zgb + bg_ref[:, d_model:2 * d_model]).astype(gb_ref.dtype)

    if feature_major:
        carry_ref[...] = jnp.broadcast_to(carry, carry_ref.shape)


def _in_proj(x2d, p, *, tm, seq_len, feature_major, q_dtype):
    n, d_model = x2d.shape
    assert n % tm == 0
    n_tiles = n // tm
    sub_rows = _pick_tile(tm, IN_PROJ_SUB_ROWS)
    A = ATTN_WIDTH
    wcols = p["w_cat"].shape[1]

    row = lambda c: pl.BlockSpec((tm, c), lambda i: (i, 0))
    in_specs = [row(d_model), _const_spec((1, d_model)), _const_spec((d_model, wcols)),
                _const_spec((1, 2 * d_model)), _const_spec((1, A)), _const_spec((A, A))]
    args = [x2d, p["g1"], p["w_cat"], p["b_gate"], p["qg_log2"] if feature_major else p["qg"], p["head_mean"]]
    tok = lambda c, dt: jax.ShapeDtypeStruct((n, c), dt)
    tail_shapes = [tok(A, F32), tok(d_model, BF16), tok(d_model, BF16)]
    tail_specs = [row(A), row(d_model), row(d_model)]
    scratch = []
    tiles_per_seq = 1
    if feature_major:
        assert seq_len % tm == 0
        tiles_per_seq = seq_len // tm
        batch = n // seq_len
        in_specs += [_const_spec((2 * A, d_model)), _const_spec((A, 1)), _const_spec((N_HEADS, d_model)),
                     _const_spec((N_HEADS, 1)), _const_spec((sub_rows, sub_rows))]
        args += [p["w_kv_t"], p["kg_t"], p["w_f_t"], p["b_f_t"], _upper_incl(sub_rows)]
        fm = lambda r: pl.BlockSpec((None, r, tm), lambda i: (i // tiles_per_seq, 0, i % tiles_per_seq))
        fms = lambda r: jax.ShapeDtypeStruct((batch, r, seq_len), F32)
        out_shape = [tok(A, q_dtype), fms(A), fms(A), fms(N_HEADS)] + tail_shapes
        out_shape.append(jax.ShapeDtypeStruct((batch * N_HEADS, seq_len), F32))
        out_specs = [row(A), fm(A), fm(A), fm(N_HEADS)] + tail_specs
        out_specs.append(pl.BlockSpec((N_HEADS, tm), lambda i: (i // tiles_per_seq, i % tiles_per_seq)))
        scratch.append(pltpu.VMEM((N_HEADS, LANES), F32))
    else:
        in_specs += [_const_spec((d_model, N_HEADS)), _const_spec((1, N_HEADS)), _const_spec((1, A))]
        args += [p["w_f"], p["b_f"], p["kg"]]
        out_shape = [tok(A, q_dtype), tok(A, F32), tok(A, F32), tok(N_HEADS, F32)] + tail_shapes
        out_specs = [row(A), row(A), row(A), row(N_HEADS)] + tail_specs

    return pl.pallas_call(
        functools.partial(_in_proj_body, tiles_per_seq=tiles_per_seq, feature_major=feature_major,
                          sub_rows=sub_rows),
        grid=(n_tiles,),
        in_specs=in_specs,
        out_specs=out_specs,
        out_shape=out_shape,
        scratch_shapes=scratch,
        compiler_params=pltpu.CompilerParams(dimension_semantics=("arbitrary",),
                                             vmem_limit_bytes=VMEM_LIMIT_BYTES),
        name="in_proj_fm" if feature_major else "in_proj",
    )(*args)


def _attn_prompt_body(q_ref, kt_ref, vt_ref, c_ref, o_ref, kb_ref, vb_ref, *, tq):
    hp = pl.program_id(1)
    seq = q_ref.shape[0]
    vb_ref[...] = vt_ref[...].astype(BF16)
    lane = lax.broadcasted_iota(jnp.int32, (1, LANES), 1)
    first_head = lane < HEAD_DIM
    causal = (lax.broadcasted_iota(jnp.int32, (tq, tq), 1)
              <= lax.broadcasted_iota(jnp.int32, (tq, tq), 0))
    kt = kt_ref[...]
    c_all = c_ref[...] * LOG2E
    sub = lax.broadcasted_iota(jnp.int32, c_all.shape, 0)
    q_fill = []
    for hh in range(HEADS_PER_VREG):
        c_row = jnp.sum(jnp.where(sub == HEADS_PER_VREG * hp + hh, c_all, 0.0), axis=0, keepdims=True)
        hi, mid, lo = (part.astype(F32) for part in _split3(-c_row))
        base = (1 - hh) * HEAD_DIM
        blk = kt[base:base + SUBLANES, :]
        blk = jnp.where(sub == 0, hi, jnp.where(sub == 1, mid, jnp.where(sub == 2, lo, blk)))
        pieces = [kt[0:base, :]] * (base > 0) + [blk, kt[base + SUBLANES:, :]]
        kb_ref[hh] = jnp.concatenate(pieces, axis=0).astype(BF16)
        q_fill.append(jnp.where((lane >= base) & (lane < base + 3), 1.0, 0.0).astype(BF16))

    for i in range(seq // tq):
        q0 = i * tq
        q = q_ref[q0:q0 + tq, :]
        outs = []
        for hh in range(HEADS_PER_VREG):
            own = first_head if hh == 0 else jnp.logical_not(first_head)
            qm = jnp.where(own, q, q_fill[hh])
            m = jnp.full((tq, 1), NEG_BIG, F32)
            l = jnp.zeros((tq, 1), F32)
            acc = jnp.zeros((tq, LANES), F32)
            for j in range(i + 1):
                k0 = j * tq
                s = _dot(qm, kb_ref[hh, :, k0:k0 + tq])
                if j == i:
                    s = jnp.where(causal, s, NEG_BIG)
                m_new = jnp.maximum(m, jnp.max(s, axis=1, keepdims=True))
                alpha = jnp.exp2(m - m_new)
                pr = jnp.exp2(s - m_new)
                l = alpha * l + jnp.sum(pr, axis=1, keepdims=True)
                acc = alpha * acc + _dot_nt(pr.astype(BF16), vb_ref[:, k0:k0 + tq])
                m = m_new
            outs.append(acc * (1.0 / l))
        o_ref[q0:q0 + tq, :] = jnp.where(first_head, outs[0], outs[1]).astype(o_ref.dtype)


def _attn_prompt(q2d, kt, vt, ct, *, tq):
    batch, _, seq_len = kt.shape
    n = q2d.shape[0]
    pairs = N_HEADS // HEADS_PER_VREG
    fm = pl.BlockSpec((None, LANES, seq_len), lambda b, hp: (b, hp, 0))
    tok = pl.BlockSpec((seq_len, LANES), lambda b, hp: (b, hp))
    return pl.pallas_call(
        functools.partial(_attn_prompt_body, tq=tq),
        grid=(batch, pairs),
        in_specs=[tok, fm, fm, pl.BlockSpec((N_HEADS, seq_len), lambda b, hp: (b, 0))],
        out_specs=tok,
        out_shape=jax.ShapeDtypeStruct((n, ATTN_WIDTH), BF16),
        scratch_shapes=[pltpu.VMEM((HEADS_PER_VREG, LANES, seq_len), BF16), pltpu.VMEM((LANES, seq_len), BF16)],
        compiler_params=pltpu.CompilerParams(dimension_semantics=("arbitrary", "arbitrary"),
                                             vmem_limit_bytes=VMEM_LIMIT_BYTES),
        name="attn_prompt",
    )(q2d, kt, vt, ct)


def _attn_sample_body(pt_ref, q_ref, ks_ref, vs_ref, lfs_ref, ckt_hbm, cvt_hbm, clt_hbm,
                      lincl_ref, sel_ref, uincl_ref, o_ref,
                      qbd_ref, m_ref, l_ref, acc_ref, tot_ref, lfpad_ref, new_ref,
                      kbuf, vbuf, lbuf, sem, *, n_pg, n_pages, dec_seq):
    g = pl.program_id(1)
    steps = pl.num_programs(1)
    last_step = pl.num_programs(0) * steps - 1
    step = pl.program_id(0) * steps + g
    slot = step % 2

    def page_copies(of_step, into_slot, lookup=True):
        base = (of_step // steps) * n_pages + n_pages - (of_step % steps + 1) * n_pg
        copies = []
        for i in range(n_pg):
            idx = pt_ref[base + i] if lookup else 0
            copies.append(pltpu.make_async_copy(ckt_hbm.at[idx], kbuf.at[into_slot, i], sem.at[into_slot, 0]))
            copies.append(pltpu.make_async_copy(cvt_hbm.at[idx], vbuf.at[into_slot, i], sem.at[into_slot, 1]))
            copies.append(pltpu.make_async_copy(clt_hbm.at[idx], lbuf.at[into_slot, i], sem.at[into_slot, 2]))
        return copies

    def start_all(copies):
        for n, cp in enumerate(copies):
            cp.start(priority=n % 2)

    @pl.when(step == 0)
    def _():
        start_all(page_copies(0, 0))

    rows = N_HEADS * dec_seq
    page = lfpad_ref.shape[0]
    row_head = lax.broadcasted_iota(jnp.int32, (rows, ATTN_WIDTH), 0) // dec_seq
    col_head = lax.broadcasted_iota(jnp.int32, (rows, ATTN_WIDTH), 1) // HEAD_DIM
    own_head = row_head == col_head

    def softmax_update(s, pv_fn):
        m = m_ref[...]
        m_new = jnp.maximum(m, jnp.max(s, axis=1, keepdims=True))
        alpha = jnp.exp(m - m_new)
        pr = jnp.exp(s - m_new)
        l_ref[...] = alpha * l_ref[...] + jnp.sum(pr, axis=1, keepdims=True)
        acc_ref[...] = alpha * acc_ref[...] + pv_fn(pr.astype(BF16))
        m_ref[...] = m_new

    @pl.when(g == 0)
    def _():
        q = q_ref[...]
        q_rows = jnp.concatenate([q] * N_HEADS, axis=0)
        qbd_ref[...] = jnp.where(own_head, q_rows, 0.0).astype(BF16)
        m_ref[...] = jnp.full(m_ref.shape, NEG_BIG, F32)
        l_ref[...] = jnp.zeros(l_ref.shape, F32)
        acc_ref[...] = jnp.zeros(acc_ref.shape, F32)
        tot_ref[...] = jnp.zeros(tot_ref.shape, F32)
        key = lax.broadcasted_iota(jnp.int32, (rows, page), 1)
        t = lax.broadcasted_iota(jnp.int32, (rows, page), 0) % dec_seq
        for slot, src in ((0, ks_ref), (1, vs_ref)):
            new_ref[slot] = jnp.zeros(new_ref.shape[1:], F32)
            new_ref[slot, 0:dec_seq, :] = src[...]
        k_new = new_ref[0].astype(BF16)
        v_new = new_ref[1].astype(BF16)
        lfpad_ref[...] = jnp.zeros(lfpad_ref.shape, F32)
        lfpad_ref[0:dec_seq, 0:N_HEADS] = lfs_ref[...]
        lincl = lincl_ref[...]
        pre = jnp.zeros((page, LANES), F32)
        for part in _split3(lfpad_ref[...]):
            pre = pre + _dot(lincl, part)
        sel = sel_ref[...]
        pre_rows = jnp.zeros((rows, page), F32)
        for part in _split3(pre):
            pre_rows = pre_rows + _dot_nt(sel, part)
        tot_new = pre_rows[:, page - 1:page]
        s = _dot_nt(qbd_ref[...], k_new) + (tot_new - pre_rows)
        softmax_update(jnp.where(key <= t, s, NEG_BIG), lambda pr: _dot(pr, v_new))
        tot_ref[...] = tot_new

    for cp in page_copies(step, slot, lookup=False):
        cp.wait()
    start_all(page_copies(jnp.minimum(step + 1, last_step), 1 - slot))
    kp = [kbuf.at[slot, i] for i in range(n_pg)]
    vp = [vbuf.at[slot, i] for i in range(n_pg)]
    lp = [lbuf.at[slot, i] for i in range(n_pg)]

    lft = jnp.concatenate([lp[i][...] for i in range(n_pg)], axis=0)
    uincl = uincl_ref[...]
    pre_all = jnp.zeros(lft.shape, F32)
    for part in _split3(lft):
        pre_all = pre_all + _dot(part, uincl)
    qbd = qbd_ref[...]
    run = tot_ref[...]
    scores = [None] * n_pg
    for i in reversed(range(n_pg)):
        pre_rows = jnp.concatenate(
            [jnp.broadcast_to(pre_all[i * N_HEADS + hd:i * N_HEADS + hd + 1, :], (dec_seq, page))
             for hd in range(N_HEADS)], axis=0)
        run = run + pre_rows[:, page - 1:page]
        scores[i] = _dot(qbd, kp[i][...].astype(BF16)) + (run - pre_rows)
    tot_ref[...] = run

    def pv_pages(pr):
        out = _dot_nt(pr[:, 0:page], vp[0][...].astype(BF16))
        for i in range(1, n_pg):
            out = out + _dot_nt(pr[:, i * page:(i + 1) * page], vp[i][...].astype(BF16))
        return out

    softmax_update(jnp.concatenate(scores, axis=1), pv_pages)

    @pl.when(g == pl.num_programs(1) - 1)
    def _():
        o = jnp.where(own_head, acc_ref[...] * (1.0 / l_ref[...]), 0.0)
        out = o[0:dec_seq, :]
        for hd in range(1, N_HEADS):
            out = out + o[hd * dec_seq:(hd + 1) * dec_seq, :]
        o_ref[...] = out

    @pl.when(step == last_step)
    def _():
        for cp in page_copies(last_step, 1 - slot, lookup=False):
            cp.wait()


def _attn_sample(q3, k3, v3, lf3, cache_kt, cache_vt, cache_lft, page_table, *, n_pg):
    dec_batch, dec_seq, _ = q3.shape
    n_pool, _, page = cache_kt.shape
    n_pages = page_table.shape[1]
    assert n_pages % n_pg == 0 and dec_seq == SUBLANES and page == LANES
    steps = n_pages // n_pg
    rows = N_HEADS * dec_seq

    uincl = _upper_incl(page)
    sel = (lax.broadcasted_iota(jnp.int32, (rows, LANES), 0) // dec_seq
           == lax.broadcasted_iota(jnp.int32, (rows, LANES), 1)).astype(BF16)

    per_b = lambda c: pl.BlockSpec((None, dec_seq, c), lambda b, g, pt: (b, 0, 0))
    const = lambda shape: pl.BlockSpec(shape, lambda b, g, pt: (0, 0))
    in_hbm = pl.BlockSpec(memory_space=pl.ANY)
    in_specs = [per_b(ATTN_WIDTH), per_b(ATTN_WIDTH), per_b(ATTN_WIDTH), per_b(N_HEADS)]
    in_specs += [in_hbm, in_hbm, in_hbm]
    in_specs += [const((page, page)), const((rows, LANES)), const((page, page))]
    grid_spec = pltpu.PrefetchScalarGridSpec(
        num_scalar_prefetch=1,
        grid=(dec_batch, steps),
        in_specs=in_specs,
        out_specs=pl.BlockSpec((None, dec_seq, ATTN_WIDTH), lambda b, g, pt: (b, 0, 0)),
        scratch_shapes=[pltpu.VMEM((rows, ATTN_WIDTH), BF16), pltpu.VMEM((rows, 1), F32),
                        pltpu.VMEM((rows, 1), F32), pltpu.VMEM((rows, ATTN_WIDTH), F32),
                        pltpu.VMEM((rows, 1), F32), pltpu.VMEM((page, LANES), F32),
                        pltpu.VMEM((2, page, ATTN_WIDTH), F32),
                        pltpu.VMEM((2, n_pg, ATTN_WIDTH, page), F32), pltpu.VMEM((2, n_pg, ATTN_WIDTH, page), F32),
                        pltpu.VMEM((2, n_pg, N_HEADS, page), F32), pltpu.SemaphoreType.DMA((2, 3))],
    )
    args = [page_table.reshape(-1), q3, k3, v3, lf3, cache_kt, cache_vt, cache_lft, uincl.T, sel, uincl]
    return pl.pallas_call(
        functools.partial(_attn_sample_body, n_pg=n_pg, n_pages=n_pages, dec_seq=dec_seq),
        grid_spec=grid_spec,
        out_shape=jax.ShapeDtypeStruct((dec_batch, dec_seq, ATTN_WIDTH), F32),
        compiler_params=pltpu.CompilerParams(dimension_semantics=("arbitrary", "arbitrary"),
                                             vmem_limit_bytes=VMEM_LIMIT_BYTES),
        name="attn_sample",
    )(*args)


def _ln_swish(c, g, b):
    mu = jnp.mean(c, axis=-1, keepdims=True)
    d = c - mu
    var = jnp.mean(d * d, axis=-1, keepdims=True)
    y = d * lax.rsqrt(var + EPS) * g + b
    return y * jax.nn.sigmoid(y)


HIST_ROWS = 32
HIST_PAD = HIST_ROWS - (CONV_WIDTH - 1)
TAP_GROUPS = -(-(HIST_PAD + CONV_WIDTH) // SUBLANES)


def _conv_prompt_body(u_ref, w_ref, b_ref, g_ref, beta_ref, o_ref, ext_ref, *, rows_per_chunk):
    tc, ch = u_ref.shape
    R = rows_per_chunk

    @pl.when(pl.program_id(1) == 0)
    def _():
        ext_ref[0:HIST_ROWS, :] = jnp.zeros((HIST_ROWS, ch), F32)
        ext_ref[HIST_ROWS + tc:HIST_ROWS + tc + SUBLANES, :] = jnp.zeros((SUBLANES, ch), F32)

    ext_ref[HIST_ROWS:HIST_ROWS + tc, :] = u_ref[...]

    for r0 in range(0, tc, R):
        cols = []
        for cb in range(ch // LANES):
            lanes = slice(cb * LANES, (cb + 1) * LANES)
            y = jnp.zeros((R, LANES), F32) + b_ref[:, lanes]
            for r in range(SUBLANES):
                part = None
                for a in range(TAP_GROUPS):
                    k = SUBLANES * a + r - HIST_PAD
                    if 0 <= k < CONV_WIDTH:
                        term = ext_ref[r0 + SUBLANES * a:r0 + SUBLANES * a + R + SUBLANES, lanes] * w_ref[k:k + 1, lanes]
                        part = term if part is None else part + term
                y = y + part[r:r + R, :]
            cols.append(y)
        acc = jnp.concatenate(cols, axis=1)
        o_ref[r0:r0 + R, :] = _ln_swish(acc, g_ref[...], beta_ref[...]).astype(o_ref.dtype)
    ext_ref[0:HIST_ROWS, :] = ext_ref[tc:tc + HIST_ROWS, :]


def _conv_prompt(u2d, p, *, batch, seq_len, tc):
    n, ch = u2d.shape
    tiles = seq_len // tc
    return pl.pallas_call(
        functools.partial(_conv_prompt_body, rows_per_chunk=_pick_tile(tc, 128)),
        grid=(batch, tiles),
        in_specs=[pl.BlockSpec((tc, ch), lambda b, i: (b * tiles + i, 0)),
                  _const_spec((CONV_WIDTH, ch)), _const_spec((1, ch)), _const_spec((1, ch)),
                  _const_spec((1, ch))],
        out_specs=pl.BlockSpec((tc, ch), lambda b, i: (b * tiles + i, 0)),
        out_shape=jax.ShapeDtypeStruct((n, ch), BF16),
        scratch_shapes=[pltpu.VMEM((HIST_ROWS + tc + SUBLANES, ch), F32)],
        compiler_params=pltpu.CompilerParams(dimension_semantics=("arbitrary", "arbitrary"),
                                             vmem_limit_bytes=VMEM_LIMIT_BYTES),
        name="conv_prompt",
    )(u2d, p["conv_w"], p["conv_b"], p["ln_g"], p["ln_b"])


def _conv_sample_body(st_ref, u_ref, w_ref, b_ref, g_ref, beta_ref, o_ref, ns_ref, ext_ref):
    hist = CONV_WIDTH - 1
    dec_seq = u_ref.shape[1]

    def one_seq(sb, _):
        ext_ref[0:hist, :] = st_ref[sb]
        ext_ref[hist:hist + dec_seq, :] = u_ref[sb]
        acc = jnp.zeros((dec_seq, ext_ref.shape[1]), F32) + b_ref[...]
        for k in range(CONV_WIDTH):
            acc = acc + ext_ref[k:k + dec_seq, :] * w_ref[k:k + 1, :]
        o_ref[sb] = _ln_swish(acc, g_ref[...], beta_ref[...]).astype(o_ref.dtype)
        ns_ref[sb] = ext_ref[dec_seq:dec_seq + hist, :]
        return 0

    lax.fori_loop(0, u_ref.shape[0], one_seq, 0)


def _conv_sample(state, u3, p, *, seqs_per_step):
    dec_batch, dec_seq, ch = u3.shape
    hist = CONV_WIDTH - 1
    blk = lambda r: pl.BlockSpec((seqs_per_step, r, ch), lambda i: (i, 0, 0))
    return pl.pallas_call(
        _conv_sample_body,
        grid=(dec_batch // seqs_per_step,),
        in_specs=[blk(hist), blk(dec_seq), _const_spec((CONV_WIDTH, ch)), _const_spec((1, ch)),
                  _const_spec((1, ch)), _const_spec((1, ch))],
        out_specs=[blk(dec_seq), blk(hist)],
        out_shape=[jax.ShapeDtypeStruct((dec_batch, dec_seq, ch), F32),
                   jax.ShapeDtypeStruct((dec_batch, hist, ch), F32)],
        scratch_shapes=[pltpu.VMEM((hist + dec_seq + 2, ch), F32)],
        compiler_params=pltpu.CompilerParams(dimension_semantics=("arbitrary",),
                                             vmem_limit_bytes=VMEM_LIMIT_BYTES),
        name="conv_sample",
    )(state, u3, p["conv_w"], p["conv_b"], p["ln_g"], p["ln_b"])


def _merge_mlp_body(x_ref, o_ref, ac_ref, ga_ref, gb_ref, wao_ref, wco_ref, wout_ref, g2_ref,
                    wup_ref, wdn_ref, y_ref, *, ff_chunk):
    ya = _dot(o_ref[...].astype(BF16), wao_ref[...])
    yc = _dot(ac_ref[...].astype(BF16), wco_ref[...])
    mix = ga_ref[...].astype(F32) * ya + gb_ref[...].astype(F32) * yc
    x1 = x_ref[...] + _dot(mix.astype(BF16), wout_ref[...])
    ms = jnp.mean(x1 * x1, axis=-1, keepdims=True)
    h2 = (x1 * lax.rsqrt(ms + EPS) * g2_ref[...]).astype(BF16)
    d_ff = wup_ref.shape[1]
    acc = x1
    for c in range(d_ff // ff_chunk):
        a = jnp.maximum(_dot(h2, wup_ref[:, c * ff_chunk:(c + 1) * ff_chunk]), 0.0)
        acc = acc + _dot((a * a).astype(BF16), wdn_ref[c * ff_chunk:(c + 1) * ff_chunk, :])
    y_ref[...] = acc


def _merge_mlp(x2d, o2d, ac2d, ga, gb, p, *, tm):
    n, d_model = x2d.shape
    d_ff = p["w_up"].shape[1]
    ch = ac2d.shape[1]
    row = lambda c: pl.BlockSpec((tm, c), lambda i: (i, 0))
    return pl.pallas_call(
        functools.partial(_merge_mlp_body, ff_chunk=512),
        grid=(n // tm,),
        in_specs=[row(d_model), row(ATTN_WIDTH), row(ch), row(d_model), row(d_model),
                  _const_spec((ATTN_WIDTH, d_model)), _const_spec((ch, d_model)),
                  _const_spec((d_model, d_model)), _const_spec((1, d_model)),
                  _const_spec((d_model, d_ff)), _const_spec((d_ff, d_model))],
        out_specs=row(d_model),
        out_shape=jax.ShapeDtypeStruct((n, d_model), F32),
        compiler_params=pltpu.CompilerParams(dimension_semantics=("arbitrary",),
                                             vmem_limit_bytes=VMEM_LIMIT_BYTES),
        name="merge_mlp",
    )(x2d, o2d, ac2d, ga, gb, p["w_attn_out"], p["w_conv_out"], p["w_out"], p["g2"], p["w_up"], p["w_down"])


def _prep_layer_params(norm1_g, w_in, b_forget, b_gate, q_norm_g, k_norm_g, w_attn_out, conv_dw_w,
                       conv_dw_b, conv_ln_g, conv_ln_b, w_conv_out, w_out, norm2_g, w_up, w_down):
    A = ATTN_WIDTH
    d_model = w_in.shape[0]
    ch = conv_dw_w.shape[1]
    f0 = 3 * A
    a0 = f0 + N_HEADS
    w_bf = w_in.astype(BF16)
    w_f = w_bf[:, f0:a0]
    w_cat = jnp.concatenate([w_bf[:, 0:f0], w_bf[:, a0:]], axis=1)
    hm = (lax.broadcasted_iota(jnp.int32, (A, A), 0) // HEAD_DIM
          == lax.broadcasted_iota(jnp.int32, (A, A), 1) // HEAD_DIM).astype(BF16) * (1.0 / HEAD_DIM)
    kg = jnp.tile(k_norm_g, N_HEADS)
    return dict(
        g1=norm1_g.reshape(1, d_model), w_cat=w_cat, w_f=w_f, w_f_t=w_f.T, w_kv_t=w_bf[:, A:f0].T,
        b_f=b_forget.reshape(1, N_HEADS), b_f_t=b_forget.reshape(N_HEADS, 1),
        b_gate=b_gate.reshape(1, 2 * d_model),
        qg=jnp.tile(q_norm_g, N_HEADS).reshape(1, A) * (HEAD_DIM ** -0.5),
        qg_log2=jnp.tile(q_norm_g, N_HEADS).reshape(1, A) * (HEAD_DIM ** -0.5 * LOG2E),
        kg=kg.reshape(1, A), kg_t=kg.reshape(A, 1),
        head_mean=hm.astype(BF16),
        conv_w=conv_dw_w, conv_b=conv_dw_b.reshape(1, ch), ln_g=conv_ln_g.reshape(1, ch),
        ln_b=conv_ln_b.reshape(1, ch),
        w_attn_out=w_attn_out.astype(BF16), w_conv_out=w_conv_out.astype(BF16),
        w_out=w_out.astype(BF16), g2=norm2_g.reshape(1, d_model),
        w_up=w_up.astype(BF16), w_down=w_down.astype(BF16))


def _pick_tile(n, pref):
    t = min(n, pref)
    while n % t:
        t //= 2
    return t


def _layer(xp, xs, cache_k, cache_v, cache_lf, state_conv, page_table, p):
    batch, seq_len, d_model = xp.shape
    dec_batch, dec_seq, _ = xs.shape
    n_pool, page, _, _ = cache_k.shape
    hist = CONV_WIDTH - 1

    n_p = batch * seq_len
    tm = _pick_tile(seq_len, 512)
    q, kt, vt, lft, u, ga, gb, ct = _in_proj(xp.reshape(n_p, d_model), p, tm=_pick_tile(seq_len, 1024),
                                             seq_len=seq_len, feature_major=True, q_dtype=BF16)
    o = _attn_prompt(q, kt, vt, ct, tq=_pick_tile(seq_len, 512))
    ac = _conv_prompt(u, p, batch=batch, seq_len=seq_len, tc=tm)
    yp = _merge_mlp(xp.reshape(n_p, d_model), o, ac, ga, gb, p, tm=tm).reshape(batch, seq_len, d_model)
    to_tok = lambda a: jnp.transpose(a.reshape(batch, N_HEADS, HEAD_DIM, seq_len), (0, 3, 1, 2))
    kp, vp = to_tok(kt), to_tok(vt)
    lp = jnp.transpose(lft, (0, 2, 1))
    cp = u.reshape(batch, seq_len, -1)[:, seq_len - hist:, :]

    n_s = dec_batch * dec_seq
    tms = _pick_tile(n_s, 512)
    qs, ks, vs, lfs, us, gas, gbs = _in_proj(xs.reshape(n_s, d_model), p, tm=tms, seq_len=dec_seq,
                                             feature_major=False, q_dtype=F32)
    r3 = lambda a: a.reshape(dec_batch, dec_seq, a.shape[-1])
    fm_cache = lambda c: jnp.transpose(c, (0, 2, 3, 1)).reshape(n_pool, ATTN_WIDTH, page)
    os_ = _attn_sample(r3(qs), r3(ks), r3(vs), r3(lfs), fm_cache(cache_k), fm_cache(cache_v),
                       jnp.transpose(cache_lf, (0, 2, 1)), page_table,
                       n_pg=_pick_tile(page_table.shape[1], 16))
    acs, new_state = _conv_sample(state_conv, r3(us), p, seqs_per_step=_pick_tile(dec_batch, 8))
    ys = _merge_mlp(xs.reshape(n_s, d_model), os_.reshape(n_s, ATTN_WIDTH), acs.reshape(n_s, -1),
                    gas, gbs, p, tm=tms).reshape(dec_batch, dec_seq, d_model)
    k_s = ks.reshape(dec_batch, dec_seq, N_HEADS, HEAD_DIM)
    v_s = vs.reshape(dec_batch, dec_seq, N_HEADS, HEAD_DIM)
    l_s = lfs.reshape(dec_batch, dec_seq, N_HEADS)
    return yp, ys, kp, vp, lp, cp, k_s, v_s, l_s, new_state


def kernel(x_prompt, x_sample, cache_k, cache_v, cache_logf, state_conv, page_table, norm1_g, w_in, b_forget, b_gate, q_norm_g, k_norm_g, w_attn_out, conv_dw_w, conv_dw_b, conv_ln_g, conv_ln_b, w_conv_out, w_out, norm2_g, w_up, w_down):
    depth = w_in.shape[0]
    xp, xs = x_prompt, x_sample
    outs = [[] for _ in range(8)]
    for l in range(depth):
        p = _prep_layer_params(norm1_g[l], w_in[l], b_forget[l], b_gate[l], q_norm_g[l], k_norm_g[l],
                               w_attn_out[l], conv_dw_w[l], conv_dw_b[l], conv_ln_g[l], conv_ln_b[l],
                               w_conv_out[l], w_out[l], norm2_g[l], w_up[l], w_down[l])
        xp, xs, *rest = _layer(xp, xs, cache_k[l], cache_v[l], cache_logf[l], state_conv[l], page_table, p)
        for acc, r in zip(outs, rest):
            acc.append(r)
    return (xp, xs) + tuple(jnp.stack(o) for o in outs)
```

```python
import functools

import jax
import jax.numpy as jnp
from jax import lax
from jax.experimental import pallas as pl
from jax.experimental.pallas import tpu as pltpu

N_HEADS = 8
HEAD_DIM = 64
ATTN_WIDTH = N_HEADS * HEAD_DIM
CONV_WIDTH = 31
EPS = 1e-6
NEG_BIG = -1e30
LOG2E = 1.4426950408889634

LANES = 128
SUBLANES = 8
HEADS_PER_VREG = LANES // HEAD_DIM
IN_PROJ_SUB_ROWS = 512
VMEM_LIMIT_BYTES = 56 * 1024 * 1024

F32 = jnp.float32
BF16 = jnp.bfloat16

NT_DIMS = (((1,), (1,)), ((), ()))


def _const_spec(shape):
    nd = len(shape)
    return pl.BlockSpec(shape, lambda *_: (0,) * nd, pipeline_mode=pl.Buffered(1))


def _split3(x):
    hi = x.astype(BF16)
    r = x - hi.astype(F32)
    mid = r.astype(BF16)
    lo = (r - mid.astype(F32)).astype(BF16)
    return hi, mid, lo


def _log_sigmoid(x):
    return jnp.minimum(x, 0.0) - jnp.log1p(jnp.exp(-jnp.abs(x)))


def _dot(a, b):
    return jnp.dot(a, b, preferred_element_type=F32)


def _dot_nt(a, b):
    return lax.dot_general(a, b, NT_DIMS, preferred_element_type=F32)


def _upper_incl(n):
    return (lax.broadcasted_iota(jnp.int32, (n, n), 0) <= lax.broadcasted_iota(jnp.int32, (n, n), 1)).astype(BF16)


def _in_proj_body(x_ref, g1_ref, w_ref, bg_ref, qg_ref, hm_ref, *rest, tiles_per_seq, feature_major, sub_rows):
    if feature_major:
        (wkvt_ref, kgt_ref, wft_ref, bft_ref, tri_ref,
         q_ref, kt_ref, vt_ref, lft_ref, u_ref, ga_ref, gb_ref, ct_ref, carry_ref) = rest

        @pl.when(pl.program_id(0) % tiles_per_seq == 0)
        def _():
            carry_ref[...] = jnp.zeros_like(carry_ref)

        carry = carry_ref[:, 0:1]
    else:
        wf_ref, bf_ref, kg_ref, q_ref, k_ref, v_ref, lf_ref, u_ref, ga_ref, gb_ref = rest

    A = ATTN_WIDTH
    tm, d_model = x_ref.shape
    hm = hm_ref[...]

    def head_rms(z, g):
        msq = _dot((z * z).astype(BF16), hm)
        return z * lax.rsqrt(msq + EPS) * g

    for r0 in range(0, tm, sub_rows):
        rs = slice(r0, r0 + sub_rows)
        x = x_ref[rs, :]
        ms = jnp.mean(x * x, axis=-1, keepdims=True)
        h = (x * lax.rsqrt(ms + EPS) * g1_ref[...]).astype(BF16)

        zq = _dot(h, w_ref[:, 0:A])
        q_ref[rs, :] = head_rms(zq, qg_ref[...]).astype(q_ref.dtype)

        if feature_major:
            zkt = _dot_nt(wkvt_ref[0:A, :], h)
            for hd in range(N_HEADS):
                rows = slice(hd * HEAD_DIM, (hd + 1) * HEAD_DIM)
                zh = zkt[rows, :]
                msq = jnp.mean(zh * zh, axis=0, keepdims=True)
                kt_ref[rows, rs] = zh * lax.rsqrt(msq + EPS) * kgt_ref[rows, :]
            vt_ref[:, rs] = _dot_nt(wkvt_ref[A:2 * A, :], h)

            lft = _log_sigmoid(_dot_nt(wft_ref[...], h) + bft_ref[...])
            lft_ref[:, rs] = lft
            tri = tri_ref[...]
            c = carry
            for part in _split3(lft):
                c = c + _dot(part, tri)
            ct_ref[:, rs] = c
            carry = c[:, sub_rows - 1:sub_rows]
        else:
            zk = _dot(h, w_ref[:, A:2 * A])
            k_ref[rs, :] = head_rms(zk, kg_ref[...])
            v_ref[rs, :] = _dot(h, w_ref[:, 2 * A:3 * A])
            lf_ref[rs, :] = _log_sigmoid(_dot(h, wf_ref[...]) + bf_ref[...])

        za = _dot(h, w_ref[:, 3 * A:4 * A])
        zu = _dot(h, w_ref[:, 4 * A:5 * A])
        u_ref[rs, :] = za * jax.nn.sigmoid(zu)

        o = 5 * A
        zga = _dot(h, w_ref[:, o:o + d_model])
        ga_ref[rs, :] = jax.nn.sigmoid(zga + bg_ref[:, 0:d_model]).astype(ga_ref.dtype)
        zgb = _dot(h, w_ref[:, o + d_model:o + 2 * d_model])
        gb_ref[rs, :] = jax.nn.sigmoid(zgb + bg_ref[:, d_model:2 * d_model]).astype(gb_ref.dtype)

    if feature_major:
        carry_ref[...] = jnp.broadcast_to(carry, carry_ref.shape)


def _in_proj(x2d, p, *, tm, seq_len, feature_major, q_dtype):
    n, d_model = x2d.shape
    assert n % tm == 0
    n_tiles = n // tm
    sub_rows = _pick_tile(tm, IN_PROJ_SUB_ROWS)
    A = ATTN_WIDTH
    wcols = p["w_cat"].shape[1]

    row = lambda c: pl.BlockSpec((tm, c), lambda i: (i, 0))
    in_specs = [row(d_model), _const_spec((1, d_model)), _const_spec((d_model, wcols)),
                _const_spec((1, 2 * d_model)), _const_spec((1, A)), _const_spec((A, A))]
    args = [x2d, p["g1"], p["w_cat"], p["b_gate"], p["qg_log2"] if feature_major else p["qg"], p["head_mean"]]
    tok = lambda c, dt: jax.ShapeDtypeStruct((n, c), dt)
    tail_shapes = [tok(A, F32), tok(d_model, BF16), tok(d_model, BF16)]
    tail_specs = [row(A), row(d_model), row(d_model)]
    scratch = []
    tiles_per_seq = 1
    if feature_major:
        assert seq_len % tm == 0
        tiles_per_seq = seq_len // tm
        batch = n // seq_len
        in_specs += [_const_spec((2 * A, d_model)), _const_spec((A, 1)), _const_spec((N_HEADS, d_model)),
                     _const_spec((N_HEADS, 1)), _const_spec((sub_rows, sub_rows))]
        args += [p["w_kv_t"], p["kg_t"], p["w_f_t"], p["b_f_t"], _upper_incl(sub_rows)]
        fm = lambda r: pl.BlockSpec((None, r, tm), lambda i: (i // tiles_per_seq, 0, i % tiles_per_seq))
        fms = lambda r: jax.ShapeDtypeStruct((batch, r, seq_len), F32)
        out_shape = [tok(A, q_dtype), fms(A), fms(A), fms(N_HEADS)] + tail_shapes
        out_shape.append(jax.ShapeDtypeStruct((batch * N_HEADS, seq_len), F32))
        out_specs = [row(A), fm(A), fm(A), fm(N_HEADS)] + tail_specs
        out_specs.append(pl.BlockSpec((N_HEADS, tm), lambda i: (i // tiles_per_seq, i % tiles_per_seq)))
        scratch.append(pltpu.VMEM((N_HEADS, LANES), F32))
    else:
        in_specs += [_const_spec((d_model, N_HEADS)), _const_spec((1, N_HEADS)), _const_spec((1, A))]
        args += [p["w_f"], p["b_f"], p["kg"]]
        out_shape = [tok(A, q_dtype), tok(A, F32), tok(A, F32), tok(N_HEADS, F32)] + tail_shapes
        out_specs = [row(A), row(A), row(A), row(N_HEADS)] + tail_specs

    return pl.pallas_call(
        functools.partial(_in_proj_body, tiles_per_seq=tiles_per_seq, feature_major=feature_major,
                          sub_rows=sub_rows),
        grid=(n_tiles,),
        in_specs=in_specs,
        out_specs=out_specs,
        out_shape=out_shape,
        scratch_shapes=scratch,
        compiler_params=pltpu.CompilerParams(dimension_semantics=("arbitrary",),
                                             vmem_limit_bytes=VMEM_LIMIT_BYTES),
        name="in_proj_fm" if feature_major else "in_proj",
    )(*args)


def _attn_prompt_body(q_ref, kt_ref, vt_ref, c_ref, o_ref, kb_ref, vb_ref, *, tq):
    hp = pl.program_id(1)
    seq = q_ref.shape[0]
    vb_ref[...] = vt_ref[...].astype(BF16)
    lane = lax.broadcasted_iota(jnp.int32, (1, LANES), 1)
    first_head = lane < HEAD_DIM
    causal = (lax.broadcasted_iota(jnp.int32, (tq, tq), 1)
              <= lax.broadcasted_iota(jnp.int32, (tq, tq), 0))
    kt = kt_ref[...]
    c_all = c_ref[...] * LOG2E
    sub = lax.broadcasted_iota(jnp.int32, c_all.shape, 0)
    q_fill = []
    for hh in range(HEADS_PER_VREG):
        c_row = jnp.sum(jnp.where(sub == HEADS_PER_VREG * hp + hh, c_all, 0.0), axis=0, keepdims=True)
        hi, mid, lo = (part.astype(F32) for part in _split3(-c_row))
        base = (1 - hh) * HEAD_DIM
        blk = kt[base:base + SUBLANES, :]
        blk = jnp.where(sub == 0, hi, jnp.where(sub == 1, mid, jnp.where(sub == 2, lo, blk)))
        pieces = [kt[0:base, :]] * (base > 0) + [blk, kt[base + SUBLANES:, :]]
        kb_ref[hh] = jnp.concatenate(pieces, axis=0).astype(BF16)
        q_fill.append(jnp.where((lane >= base) & (lane < base + 3), 1.0, 0.0).astype(BF16))

    for i in range(seq // tq):
        q0 = i * tq
        q = q_ref[q0:q0 + tq, :]
        outs = []
        for hh in range(HEADS_PER_VREG):
            own = first_head if hh == 0 else jnp.logical_not(first_head)
            qm = jnp.where(own, q, q_fill[hh])
            m = jnp.full((tq, 1), NEG_BIG, F32)
            l = jnp.zeros((tq, 1), F32)
            acc = jnp.zeros((tq, LANES), F32)
            for j in range(i + 1):
                k0 = j * tq
                s = _dot(qm, kb_ref[hh, :, k0:k0 + tq])
                if j == i:
                    s = jnp.where(causal, s, NEG_BIG)
                m_new = jnp.maximum(m, jnp.max(s, axis=1, keepdims=True))
                alpha = jnp.exp2(m - m_new)
                pr = jnp.exp2(s - m_new)
                l = alpha * l + jnp.sum(pr, axis=1, keepdims=True)
                acc = alpha * acc + _dot_nt(pr.astype(BF16), vb_ref[:, k0:k0 + tq])
                m = m_new
            outs.append(acc * (1.0 / l))
        o_ref[q0:q0 + tq, :] = jnp.where(first_head, outs[0], outs[1]).astype(o_ref.dtype)


def _attn_prompt(q2d, kt, vt, ct, *, tq):
    batch, _, seq_len = kt.shape
    n = q2d.shape[0]
    pairs = N_HEADS // HEADS_PER_VREG
    fm = pl.BlockSpec((None, LANES, seq_len), lambda b, hp: (b, hp, 0))
    tok = pl.BlockSpec((seq_len, LANES), lambda b, hp: (b, hp))
    return pl.pallas_call(
        functools.partial(_attn_prompt_body, tq=tq),
        grid=(batch, pairs),
        in_specs=[tok, fm, fm, pl.BlockSpec((N_HEADS, seq_len), lambda b, hp: (b, 0))],
        out_specs=tok,
        out_shape=jax.ShapeDtypeStruct((n, ATTN_WIDTH), BF16),
        scratch_shapes=[pltpu.VMEM((HEADS_PER_VREG, LANES, seq_len), BF16), pltpu.VMEM((LANES, seq_len), BF16)],
        compiler_params=pltpu.CompilerParams(dimension_semantics=("arbitrary", "arbitrary"),
                                             vmem_limit_bytes=VMEM_LIMIT_BYTES),
        name="attn_prompt",
    )(q2d, kt, vt, ct)


def _attn_sample_body(pt_ref, q_ref, ks_ref, vs_ref, lfs_ref, *rest, n_pg, dec_seq, conv_steps_per_seq):
    kp = rest[0:n_pg]
    vp = rest[n_pg:2 * n_pg]
    lp = rest[2 * n_pg:3 * n_pg]
    lincl_ref, sel_ref, uincl_ref = rest[3 * n_pg:3 * n_pg + 3]
    rest = rest[3 * n_pg + 3:]
    if conv_steps_per_seq:
        (u_ref, cw_ref, cb_ref, lg_ref, lb_ref, o_ref, ac_ref,
         qbd_ref, m_ref, l_ref, acc_ref, tot_ref, lfpad_ref, new_ref, ext_ref) = rest
    else:
        o_ref, qbd_ref, m_ref, l_ref, acc_ref, tot_ref, lfpad_ref, new_ref = rest
    del pt_ref

    g = pl.program_id(1)
    if conv_steps_per_seq:
        @pl.when((pl.program_id(0) * pl.num_programs(1) + g) % conv_steps_per_seq == 0)
        def _():
            _conv_reset(ext_ref, u_ref.shape[0])

    rows = N_HEADS * dec_seq
    page = lfpad_ref.shape[0]
    row_head = lax.broadcasted_iota(jnp.int32, (rows, ATTN_WIDTH), 0) // dec_seq
    col_head = lax.broadcasted_iota(jnp.int32, (rows, ATTN_WIDTH), 1) // HEAD_DIM
    own_head = row_head == col_head

    def softmax_update(s, pv_fn):
        m = m_ref[...]
        m_new = jnp.maximum(m, jnp.max(s, axis=1, keepdims=True))
        alpha = jnp.exp(m - m_new)
        pr = jnp.exp(s - m_new)
        l_ref[...] = alpha * l_ref[...] + jnp.sum(pr, axis=1, keepdims=True)
        acc_ref[...] = alpha * acc_ref[...] + pv_fn(pr.astype(BF16))
        m_ref[...] = m_new

    @pl.when(g == 0)
    def _():
        q = q_ref[...]
        q_rows = jnp.concatenate([q] * N_HEADS, axis=0)
        qbd_ref[...] = jnp.where(own_head, q_rows, 0.0).astype(BF16)
        m_ref[...] = jnp.full(m_ref.shape, NEG_BIG, F32)
        l_ref[...] = jnp.zeros(l_ref.shape, F32)
        acc_ref[...] = jnp.zeros(acc_ref.shape, F32)
        tot_ref[...] = jnp.zeros(tot_ref.shape, F32)
        key = lax.broadcasted_iota(jnp.int32, (rows, page), 1)
        t = lax.broadcasted_iota(jnp.int32, (rows, page), 0) % dec_seq
        for slot, src in ((0, ks_ref), (1, vs_ref)):
            new_ref[slot] = jnp.zeros(new_ref.shape[1:], F32)
            new_ref[slot, 0:dec_seq, :] = src[...]
        k_new = new_ref[0].astype(BF16)
        v_new = new_ref[1].astype(BF16)
        lfpad_ref[...] = jnp.zeros(lfpad_ref.shape, F32)
        lfpad_ref[0:dec_seq, 0:N_HEADS] = lfs_ref[...]
        lincl = lincl_ref[...]
        pre = jnp.zeros((page, LANES), F32)
        for part in _split3(lfpad_ref[...]):
            pre = pre + _dot(lincl, part)
        sel = sel_ref[...]
        pre_rows = jnp.zeros((rows, page), F32)
        for part in _split3(pre):
            pre_rows = pre_rows + _dot_nt(sel, part)
        tot_new = pre_rows[:, page - 1:page]
        s = _dot_nt(qbd_ref[...], k_new) + (tot_new - pre_rows)
        softmax_update(jnp.where(key <= t, s, NEG_BIG), lambda pr: _dot(pr, v_new))
        tot_ref[...] = tot_new

    if conv_steps_per_seq:
        _conv_tile(u_ref, cw_ref, cb_ref, lg_ref, lb_ref, ac_ref, ext_ref, rows_per_chunk=u_ref.shape[0])

    lft = jnp.concatenate([lp[i][...] for i in range(n_pg)], axis=0)
    uincl = uincl_ref[...]
    pre_all = jnp.zeros(lft.shape, F32)
    for part in _split3(lft):
        pre_all = pre_all + _dot(part, uincl)
    qbd = qbd_ref[...]
    run = tot_ref[...]
    scores = [None] * n_pg
    for i in reversed(range(n_pg)):
        pre_rows = jnp.concatenate(
            [jnp.broadcast_to(pre_all[i * N_HEADS + hd:i * N_HEADS + hd + 1, :], (dec_seq, page))
             for hd in range(N_HEADS)], axis=0)
        run = run + pre_rows[:, page - 1:page]
        scores[i] = _dot(qbd, kp[i][...].astype(BF16)) + (run - pre_rows)
    tot_ref[...] = run

    def pv_pages(pr):
        out = _dot_nt(pr[:, 0:page], vp[0][...].astype(BF16))
        for i in range(1, n_pg):
            out = out + _dot_nt(pr[:, i * page:(i + 1) * page], vp[i][...].astype(BF16))
        return out

    softmax_update(jnp.concatenate(scores, axis=1), pv_pages)

    @pl.when(g == pl.num_programs(1) - 1)
    def _():
        o = jnp.where(own_head, acc_ref[...] * (1.0 / l_ref[...]), 0.0)
        out = o[0:dec_seq, :]
        for hd in range(1, N_HEADS):
            out = out + o[hd * dec_seq:(hd + 1) * dec_seq, :]
        o_ref[...] = out


def _conv_rows_per_step(n_tokens, seq_len, total_steps):
    if n_tokens % total_steps:
        return 0
    r = n_tokens // total_steps
    ok = r % (2 * SUBLANES) == 0 and seq_len % r == 0 and HIST_ROWS <= r <= 512
    return r if ok else 0


def _attn_sample(q3, k3, v3, lf3, cache_kt, cache_vt, cache_lft, page_table, *, n_pg, conv=None):
    dec_batch, dec_seq, _ = q3.shape
    n_pool, _, page = cache_kt.shape
    n_pages = page_table.shape[1]
    assert n_pages % n_pg == 0 and dec_seq == SUBLANES and page == LANES
    steps = n_pages // n_pg
    rows = N_HEADS * dec_seq

    uincl = _upper_incl(page)
    sel = (lax.broadcasted_iota(jnp.int32, (rows, LANES), 0) // dec_seq
           == lax.broadcasted_iota(jnp.int32, (rows, LANES), 1)).astype(BF16)

    def page_map(slot):
        def index_map(b, g, pt):
            return (pt[b * n_pages + n_pages - (g + 1) * n_pg + slot], 0, 0)
        return index_map

    per_b = lambda c: pl.BlockSpec((None, dec_seq, c), lambda b, g, pt: (b, 0, 0))
    const = lambda shape: pl.BlockSpec(shape, lambda b, g, pt: (0, 0))
    in_specs = [per_b(ATTN_WIDTH), per_b(ATTN_WIDTH), per_b(ATTN_WIDTH), per_b(N_HEADS)]
    in_specs += [pl.BlockSpec((None, ATTN_WIDTH, page), page_map(i)) for i in range(n_pg)]
    in_specs += [pl.BlockSpec((None, ATTN_WIDTH, page), page_map(i)) for i in range(n_pg)]
    in_specs += [pl.BlockSpec((None, N_HEADS, page), page_map(i)) for i in range(n_pg)]
    in_specs += [const((page, page)), const((rows, LANES)), const((page, page))]
    out_specs = [pl.BlockSpec((None, dec_seq, ATTN_WIDTH), lambda b, g, pt: (b, 0, 0))]
    out_shape = [jax.ShapeDtypeStruct((dec_batch, dec_seq, ATTN_WIDTH), F32)]
    scratch = [pltpu.VMEM((rows, ATTN_WIDTH), BF16), pltpu.VMEM((rows, 1), F32),
               pltpu.VMEM((rows, 1), F32), pltpu.VMEM((rows, ATTN_WIDTH), F32),
               pltpu.VMEM((rows, 1), F32), pltpu.VMEM((page, LANES), F32),
               pltpu.VMEM((2, page, ATTN_WIDTH), F32)]
    args = [page_table.reshape(-1), q3, k3, v3, lf3]
    args += [cache_kt] * n_pg + [cache_vt] * n_pg + [cache_lft] * n_pg + [uincl.T, sel, uincl]
    conv_steps_per_seq = 0
    if conv is not None:
        u2d, p, seq_len, r = conv
        ch = u2d.shape[1]
        conv_steps_per_seq = seq_len // r
        slice_spec = pl.BlockSpec((r, ch), lambda b, g, pt: (b * steps + g, 0))
        in_specs += [slice_spec, const((CONV_WIDTH, ch)), const((1, ch)), const((1, ch)), const((1, ch))]
        args += [u2d, p["conv_w"], p["conv_b"], p["ln_g"], p["ln_b"]]
        out_specs.append(slice_spec)
        out_shape.append(jax.ShapeDtypeStruct(u2d.shape, BF16))
        scratch.append(pltpu.VMEM((HIST_ROWS + r + SUBLANES, ch), F32))
    grid_spec = pltpu.PrefetchScalarGridSpec(
        num_scalar_prefetch=1,
        grid=(dec_batch, steps),
        in_specs=in_specs,
        out_specs=out_specs,
        scratch_shapes=scratch,
    )
    outs = pl.pallas_call(
        functools.partial(_attn_sample_body, n_pg=n_pg, dec_seq=dec_seq, conv_steps_per_seq=conv_steps_per_seq),
        grid_spec=grid_spec,
        out_shape=out_shape,
        compiler_params=pltpu.CompilerParams(dimension_semantics=("arbitrary", "arbitrary"),
                                             vmem_limit_bytes=VMEM_LIMIT_BYTES),
        name="attn_sample_conv" if conv is not None else "attn_sample",
    )(*args)
    return outs if conv is not None else outs[0]


def _ln_swish(c, g, b):
    mu = jnp.mean(c, axis=-1, keepdims=True)
    d = c - mu
    var = jnp.mean(d * d, axis=-1, keepdims=True)
    y = d * lax.rsqrt(var + EPS) * g + b
    return y * jax.nn.sigmoid(y)


HIST_ROWS = 32
HIST_PAD = HIST_ROWS - (CONV_WIDTH - 1)
TAP_GROUPS = -(-(HIST_PAD + CONV_WIDTH) // SUBLANES)


def _conv_reset(ext_ref, tc):
    ch = ext_ref.shape[1]
    ext_ref[0:HIST_ROWS, :] = jnp.zeros((HIST_ROWS, ch), F32)
    ext_ref[HIST_ROWS + tc:HIST_ROWS + tc + SUBLANES, :] = jnp.zeros((SUBLANES, ch), F32)


def _conv_tile(u_ref, w_ref, b_ref, g_ref, beta_ref, o_ref, ext_ref, *, rows_per_chunk):
    tc, ch = u_ref.shape
    R = rows_per_chunk
    ext_ref[HIST_ROWS:HIST_ROWS + tc, :] = u_ref[...]

    for r0 in range(0, tc, R):
        cols = []
        for cb in range(ch // LANES):
            lanes = slice(cb * LANES, (cb + 1) * LANES)
            y = jnp.zeros((R, LANES), F32) + b_ref[:, lanes]
            for r in range(SUBLANES):
                part = None
                for a in range(TAP_GROUPS):
                    k = SUBLANES * a + r - HIST_PAD
                    if 0 <= k < CONV_WIDTH:
                        term = ext_ref[r0 + SUBLANES * a:r0 + SUBLANES * a + R + SUBLANES, lanes] * w_ref[k:k + 1, lanes]
                        part = term if part is None else part + term
                y = y + part[r:r + R, :]
            cols.append(y)
        acc = jnp.concatenate(cols, axis=1)
        o_ref[r0:r0 + R, :] = _ln_swish(acc, g_ref[...], beta_ref[...]).astype(o_ref.dtype)
    ext_ref[0:HIST_ROWS, :] = ext_ref[tc:tc + HIST_ROWS, :]


def _conv_prompt_body(u_ref, w_ref, b_ref, g_ref, beta_ref, o_ref, ext_ref, *, rows_per_chunk):
    @pl.when(pl.program_id(1) == 0)
    def _():
        _conv_reset(ext_ref, u_ref.shape[0])

    _conv_tile(u_ref, w_ref, b_ref, g_ref, beta_ref, o_ref, ext_ref, rows_per_chunk=rows_per_chunk)


def _conv_prompt(u2d, p, *, batch, seq_len, tc):
    n, ch = u2d.shape
    tiles = seq_len // tc
    return pl.pallas_call(
        functools.partial(_conv_prompt_body, rows_per_chunk=_pick_tile(tc, 128)),
        grid=(batch, tiles),
        in_specs=[pl.BlockSpec((tc, ch), lambda b, i: (b * tiles + i, 0)),
                  _const_spec((CONV_WIDTH, ch)), _const_spec((1, ch)), _const_spec((1, ch)),
                  _const_spec((1, ch))],
        out_specs=pl.BlockSpec((tc, ch), lambda b, i: (b * tiles + i, 0)),
        out_shape=jax.ShapeDtypeStruct((n, ch), BF16),
        scratch_shapes=[pltpu.VMEM((HIST_ROWS + tc + SUBLANES, ch), F32)],
        compiler_params=pltpu.CompilerParams(dimension_semantics=("arbitrary", "arbitrary"),
                                             vmem_limit_bytes=VMEM_LIMIT_BYTES),
        name="conv_prompt",
    )(u2d, p["conv_w"], p["conv_b"], p["ln_g"], p["ln_b"])


def _conv_sample_body(st_ref, u_ref, w_ref, b_ref, g_ref, beta_ref, o_ref, ns_ref, ext_ref):
    hist = CONV_WIDTH - 1
    dec_seq = u_ref.shape[1]

    def one_seq(sb, _):
        ext_ref[0:hist, :] = st_ref[sb]
        ext_ref[hist:hist + dec_seq, :] = u_ref[sb]
        acc = jnp.zeros((dec_seq, ext_ref.shape[1]), F32) + b_ref[...]
        for k in range(CONV_WIDTH):
            acc = acc + ext_ref[k:k + dec_seq, :] * w_ref[k:k + 1, :]
        o_ref[sb] = _ln_swish(acc, g_ref[...], beta_ref[...]).astype(o_ref.dtype)
        ns_ref[sb] = ext_ref[dec_seq:dec_seq + hist, :]
        return 0

    lax.fori_loop(0, u_ref.shape[0], one_seq, 0)


def _conv_sample(state, u3, p, *, seqs_per_step):
    dec_batch, dec_seq, ch = u3.shape
    hist = CONV_WIDTH - 1
    blk = lambda r: pl.BlockSpec((seqs_per_step, r, ch), lambda i: (i, 0, 0))
    return pl.pallas_call(
        _conv_sample_body,
        grid=(dec_batch // seqs_per_step,),
        in_specs=[blk(hist), blk(dec_seq), _const_spec((CONV_WIDTH, ch)), _const_spec((1, ch)),
                  _const_spec((1, ch)), _const_spec((1, ch))],
        out_specs=[blk(dec_seq), blk(hist)],
        out_shape=[jax.ShapeDtypeStruct((dec_batch, dec_seq, ch), F32),
                   jax.ShapeDtypeStruct((dec_batch, hist, ch), F32)],
        scratch_shapes=[pltpu.VMEM((hist + dec_seq + 2, ch), F32)],
        compiler_params=pltpu.CompilerParams(dimension_semantics=("arbitrary",),
                                             vmem_limit_bytes=VMEM_LIMIT_BYTES),
        name="conv_sample",
    )(state, u3, p["conv_w"], p["conv_b"], p["ln_g"], p["ln_b"])


def _merge_mlp_body(x_ref, o_ref, ac_ref, ga_ref, gb_ref, wao_ref, wco_ref, wout_ref, g2_ref,
                    wup_ref, wdn_ref, y_ref, *, ff_chunk):
    ya = _dot(o_ref[...].astype(BF16), wao_ref[...])
    yc = _dot(ac_ref[...].astype(BF16), wco_ref[...])
    mix = ga_ref[...].astype(F32) * ya + gb_ref[...].astype(F32) * yc
    x1 = x_ref[...] + _dot(mix.astype(BF16), wout_ref[...])
    ms = jnp.mean(x1 * x1, axis=-1, keepdims=True)
    h2 = (x1 * lax.rsqrt(ms + EPS) * g2_ref[...]).astype(BF16)
    d_ff = wup_ref.shape[1]
    acc = x1
    for c in range(d_ff // ff_chunk):
        a = jnp.maximum(_dot(h2, wup_ref[:, c * ff_chunk:(c + 1) * ff_chunk]), 0.0)
        acc = acc + _dot((a * a).astype(BF16), wdn_ref[c * ff_chunk:(c + 1) * ff_chunk, :])
    y_ref[...] = acc


def _merge_mlp(x2d, o2d, ac2d, ga, gb, p, *, tm):
    n, d_model = x2d.shape
    d_ff = p["w_up"].shape[1]
    ch = ac2d.shape[1]
    row = lambda c: pl.BlockSpec((tm, c), lambda i: (i, 0))
    return pl.pallas_call(
        functools.partial(_merge_mlp_body, ff_chunk=512),
        grid=(n // tm,),
        in_specs=[row(d_model), row(ATTN_WIDTH), row(ch), row(d_model), row(d_model),
                  _const_spec((ATTN_WIDTH, d_model)), _const_spec((ch, d_model)),
                  _const_spec((d_model, d_model)), _const_spec((1, d_model)),
                  _const_spec((d_model, d_ff)), _const_spec((d_ff, d_model))],
        out_specs=row(d_model),
        out_shape=jax.ShapeDtypeStruct((n, d_model), F32),
        compiler_params=pltpu.CompilerParams(dimension_semantics=("arbitrary",),
                                             vmem_limit_bytes=VMEM_LIMIT_BYTES),
        name="merge_mlp",
    )(x2d, o2d, ac2d, ga, gb, p["w_attn_out"], p["w_conv_out"], p["w_out"], p["g2"], p["w_up"], p["w_down"])


def _prep_layer_params(norm1_g, w_in, b_forget, b_gate, q_norm_g, k_norm_g, w_attn_out, conv_dw_w,
                       conv_dw_b, conv_ln_g, conv_ln_b, w_conv_out, w_out, norm2_g, w_up, w_down):
    A = ATTN_WIDTH
    d_model = w_in.shape[0]
    ch = conv_dw_w.shape[1]
    f0 = 3 * A
    a0 = f0 + N_HEADS
    w_bf = w_in.astype(BF16)
    w_f = w_bf[:, f0:a0]
    w_cat = jnp.concatenate([w_bf[:, 0:f0], w_bf[:, a0:]], axis=1)
    hm = (lax.broadcasted_iota(jnp.int32, (A, A), 0) // HEAD_DIM
          == lax.broadcasted_iota(jnp.int32, (A, A), 1) // HEAD_DIM).astype(BF16) * (1.0 / HEAD_DIM)
    kg = jnp.tile(k_norm_g, N_HEADS)
    return dict(
        g1=norm1_g.reshape(1, d_model), w_cat=w_cat, w_f=w_f, w_f_t=w_f.T, w_kv_t=w_bf[:, A:f0].T,
        b_f=b_forget.reshape(1, N_HEADS), b_f_t=b_forget.reshape(N_HEADS, 1),
        b_gate=b_gate.reshape(1, 2 * d_model),
        qg=jnp.tile(q_norm_g, N_HEADS).reshape(1, A) * (HEAD_DIM ** -0.5),
        qg_log2=jnp.tile(q_norm_g, N_HEADS).reshape(1, A) * (HEAD_DIM ** -0.5 * LOG2E),
        kg=kg.reshape(1, A), kg_t=kg.reshape(A, 1),
        head_mean=hm.astype(BF16),
        conv_w=conv_dw_w, conv_b=conv_dw_b.reshape(1, ch), ln_g=conv_ln_g.reshape(1, ch),
        ln_b=conv_ln_b.reshape(1, ch),
        w_attn_out=w_attn_out.astype(BF16), w_conv_out=w_conv_out.astype(BF16),
        w_out=w_out.astype(BF16), g2=norm2_g.reshape(1, d_model),
        w_up=w_up.astype(BF16), w_down=w_down.astype(BF16))


def _pick_tile(n, pref):
    t = min(n, pref)
    while n % t:
        t //= 2
    return t


def _layer(xp, xs, cache_k, cache_v, cache_lf, state_conv, page_table, p):
    batch, seq_len, d_model = xp.shape
    dec_batch, dec_seq, _ = xs.shape
    n_pool, page, _, _ = cache_k.shape
    hist = CONV_WIDTH - 1

    n_p = batch * seq_len
    tm = _pick_tile(seq_len, 512)
    q, kt, vt, lft, u, ga, gb, ct = _in_proj(xp.reshape(n_p, d_model), p, tm=_pick_tile(seq_len, 1024),
                                             seq_len=seq_len, feature_major=True, q_dtype=BF16)
    o = _attn_prompt(q, kt, vt, ct, tq=_pick_tile(seq_len, 512))
    to_tok = lambda a: jnp.transpose(a.reshape(batch, N_HEADS, HEAD_DIM, seq_len), (0, 3, 1, 2))
    kp, vp = to_tok(kt), to_tok(vt)
    lp = jnp.transpose(lft, (0, 2, 1))
    cp = u.reshape(batch, seq_len, -1)[:, seq_len - hist:, :]

    n_s = dec_batch * dec_seq
    tms = _pick_tile(n_s, 512)
    qs, ks, vs, lfs, us, gas, gbs = _in_proj(xs.reshape(n_s, d_model), p, tm=tms, seq_len=dec_seq,
                                             feature_major=False, q_dtype=F32)
    r3 = lambda a: a.reshape(dec_batch, dec_seq, a.shape[-1])
    fm_cache = lambda c: jnp.transpose(c, (0, 2, 3, 1)).reshape(n_pool, ATTN_WIDTH, page)
    n_pg = _pick_tile(page_table.shape[1], 16)
    sample_args = (r3(qs), r3(ks), r3(vs), r3(lfs), fm_cache(cache_k), fm_cache(cache_v),
                   jnp.transpose(cache_lf, (0, 2, 1)), page_table)
    conv_rows = _conv_rows_per_step(n_p, seq_len, dec_batch * (page_table.shape[1] // n_pg))
    if conv_rows:
        os_, ac = _attn_sample(*sample_args, n_pg=n_pg, conv=(u, p, seq_len, conv_rows))
    else:
        os_ = _attn_sample(*sample_args, n_pg=n_pg)
        ac = _conv_prompt(u, p, batch=batch, seq_len=seq_len, tc=tm)
    yp = _merge_mlp(xp.reshape(n_p, d_model), o, ac, ga, gb, p, tm=tm).reshape(batch, seq_len, d_model)
    acs, new_state = _conv_sample(state_conv, r3(us), p, seqs_per_step=_pick_tile(dec_batch, 8))
    ys = _merge_mlp(xs.reshape(n_s, d_model), os_.reshape(n_s, ATTN_WIDTH), acs.reshape(n_s, -1),
                    gas, gbs, p, tm=tms).reshape(dec_batch, dec_seq, d_model)
    k_s = ks.reshape(dec_batch, dec_seq, N_HEADS, HEAD_DIM)
    v_s = vs.reshape(dec_batch, dec_seq, N_HEADS, HEAD_DIM)
    l_s = lfs.reshape(dec_batch, dec_seq, N_HEADS)
    return yp, ys, kp, vp, lp, cp, k_s, v_s, l_s, new_state


def kernel(x_prompt, x_sample, cache_k, cache_v, cache_logf, state_conv, page_table, norm1_g, w_in, b_forget, b_gate, q_norm_g, k_norm_g, w_attn_out, conv_dw_w, conv_dw_b, conv_ln_g, conv_ln_b, w_conv_out, w_out, norm2_g, w_up, w_down):
    depth = w_in.shape[0]
    xp, xs = x_prompt, x_sample
    outs = [[] for _ in range(8)]
    for l in range(depth):
        p = _prep_layer_params(norm1_g[l], w_in[l], b_forget[l], b_gate[l], q_norm_g[l], k_norm_g[l],
                               w_attn_out[l], conv_dw_w[l], conv_dw_b[l], conv_ln_g[l], conv_ln_b[l],
                               w_conv_out[l], w_out[l], norm2_g[l], w_up[l], w_down[l])
        xp, xs, *rest = _layer(xp, xs, cache_k[l], cache_v[l], cache_logf[l], state_conv[l], page_table, p)
        for acc, r in zip(outs, rest):
            acc.append(r)
    return (xp, xs) + tuple(jnp.stack(o) for o in outs)
```

```python
import functools

import jax
import jax.numpy as jnp
from jax import lax
from jax.experimental import pallas as pl
from jax.experimental.pallas import tpu as pltpu

N_HEADS = 8
HEAD_DIM = 64
ATTN_WIDTH = N_HEADS * HEAD_DIM
CONV_WIDTH = 31
EPS = 1e-6
NEG_BIG = -1e30
LOG2E = 1.4426950408889634

LANES = 128
SUBLANES = 8
HEADS_PER_VREG = LANES // HEAD_DIM
IN_PROJ_SUB_ROWS = 512
VMEM_LIMIT_BYTES = 56 * 1024 * 1024

F32 = jnp.float32
BF16 = jnp.bfloat16

NT_DIMS = (((1,), (1,)), ((), ()))


def _const_spec(shape):
    nd = len(shape)
    return pl.BlockSpec(shape, lambda *_: (0,) * nd, pipeline_mode=pl.Buffered(1))


def _split3(x):
    hi = x.astype(BF16)
    r = x - hi.astype(F32)
    mid = r.astype(BF16)
    lo = (r - mid.astype(F32)).astype(BF16)
    return hi, mid, lo


def _log_sigmoid(x):
    return jnp.minimum(x, 0.0) - jnp.log1p(jnp.exp(-jnp.abs(x)))


def _dot(a, b):
    return jnp.dot(a, b, preferred_element_type=F32)


def _dot_nt(a, b):
    return lax.dot_general(a, b, NT_DIMS, preferred_element_type=F32)


def _upper_incl(n):
    return (lax.broadcasted_iota(jnp.int32, (n, n), 0) <= lax.broadcasted_iota(jnp.int32, (n, n), 1)).astype(BF16)


def _in_proj_body(x_ref, g1_ref, w_ref, bg_ref, qg_ref, hm_ref, *rest, tiles_per_seq, feature_major, sub_rows):
    if feature_major:
        (wkvt_ref, kgt_ref, wft_ref, bft_ref, tri_ref,
         q_ref, kt_ref, vt_ref, lft_ref, u_ref, ga_ref, gb_ref, ct_ref, carry_ref) = rest

        @pl.when(pl.program_id(0) % tiles_per_seq == 0)
        def _():
            carry_ref[...] = jnp.zeros_like(carry_ref)

        carry = carry_ref[:, 0:1]
    else:
        wf_ref, bf_ref, kg_ref, q_ref, k_ref, v_ref, lf_ref, u_ref, ga_ref, gb_ref = rest

    A = ATTN_WIDTH
    tm, d_model = x_ref.shape
    hm = hm_ref[...]

    def head_rms(z, g):
        msq = _dot((z * z).astype(BF16), hm)
        return z * lax.rsqrt(msq + EPS) * g

    for r0 in range(0, tm, sub_rows):
        rs = slice(r0, r0 + sub_rows)
        x = x_ref[rs, :]
        ms = jnp.mean(x * x, axis=-1, keepdims=True)
        h = (x * lax.rsqrt(ms + EPS) * g1_ref[...]).astype(BF16)

        zq = _dot(h, w_ref[:, 0:A])
        q_ref[rs, :] = head_rms(zq, qg_ref[...]).astype(q_ref.dtype)

        if feature_major:
            zkt = _dot_nt(wkvt_ref[0:A, :], h)
            for hd in range(N_HEADS):
                rows = slice(hd * HEAD_DIM, (hd + 1) * HEAD_DIM)
                zh = zkt[rows, :]
                msq = jnp.mean(zh * zh, axis=0, keepdims=True)
                kt_ref[rows, rs] = zh * lax.rsqrt(msq + EPS) * kgt_ref[rows, :]
            vt_ref[:, rs] = _dot_nt(wkvt_ref[A:2 * A, :], h)

            lft = _log_sigmoid(_dot_nt(wft_ref[...], h) + bft_ref[...])
            lft_ref[:, rs] = lft
            tri = tri_ref[...]
            c = carry
            for part in _split3(lft):
                c = c + _dot(part, tri)
            ct_ref[:, rs] = c
            carry = c[:, sub_rows - 1:sub_rows]
        else:
            zk = _dot(h, w_ref[:, A:2 * A])
            k_ref[rs, :] = head_rms(zk, kg_ref[...])
            v_ref[rs, :] = _dot(h, w_ref[:, 2 * A:3 * A])
            lf_ref[rs, :] = _log_sigmoid(_dot(h, wf_ref[...]) + bf_ref[...])

        za = _dot(h, w_ref[:, 3 * A:4 * A])
        zu = _dot(h, w_ref[:, 4 * A:5 * A])
        u_ref[rs, :] = za * jax.nn.sigmoid(zu)

        o = 5 * A
        zga = _dot(h, w_ref[:, o:o + d_model])
        ga_ref[rs, :] = jax.nn.sigmoid(zga + bg_ref[:, 0:d_model]).astype(ga_ref.dtype)
        zgb = _dot(h, w_ref[:, o + d_model:o + 2 * d_model])
        gb_ref[rs, :] = jax.nn.sigmoid(zgb + bg_ref[:, d_model:2 * d_model]).astype(gb_ref.dtype)

    if feature_major:
        carry_ref[...] = jnp.broadcast_to(carry, carry_ref.shape)


def _in_proj(x2d, p, *, tm, seq_len, feature_major, q_dtype):
    n, d_model = x2d.shape
    assert n % tm == 0
    n_tiles = n // tm
    sub_rows = _pick_tile(tm, IN_PROJ_SUB_ROWS)
    A = ATTN_WIDTH
    wcols = p["w_cat"].shape[1]

    row = lambda c: pl.BlockSpec((tm, c), lambda i: (i, 0))
    in_specs = [row(d_model), _const_spec((1, d_model)), _const_spec((d_model, wcols)),
                _const_spec((1, 2 * d_model)), _const_spec((1, A)), _const_spec((A, A))]
    args = [x2d, p["g1"], p["w_cat"], p["b_gate"], p["qg_log2"] if feature_major else p["qg"], p["head_mean"]]
    tok = lambda c, dt: jax.ShapeDtypeStruct((n, c), dt)
    tail_shapes = [tok(A, F32), tok(d_model, BF16), tok(d_model, BF16)]
    tail_specs = [row(A), row(d_model), row(d_model)]
    scratch = []
    tiles_per_seq = 1
    if feature_major:
        assert seq_len % tm == 0
        tiles_per_seq = seq_len // tm
        batch = n // seq_len
        in_specs += [_const_spec((2 * A, d_model)), _const_spec((A, 1)), _const_spec((N_HEADS, d_model)),
                     _const_spec((N_HEADS, 1)), _const_spec((sub_rows, sub_rows))]
        args += [p["w_kv_t"], p["kg_t"], p["w_f_t"], p["b_f_t"], _upper_incl(sub_rows)]
        fm = lambda r: pl.BlockSpec((None, r, tm), lambda i: (i // tiles_per_seq, 0, i % tiles_per_seq))
        fms = lambda r: jax.ShapeDtypeStruct((batch, r, seq_len), F32)
        out_shape = [tok(A, q_dtype), fms(A), fms(A), fms(N_HEADS)] + tail_shapes
        out_shape.append(jax.ShapeDtypeStruct((batch * N_HEADS, seq_len), F32))
        out_specs = [row(A), fm(A), fm(A), fm(N_HEADS)] + tail_specs
        out_specs.append(pl.BlockSpec((N_HEADS, tm), lambda i: (i // tiles_per_seq, i % tiles_per_seq)))
        scratch.append(pltpu.VMEM((N_HEADS, LANES), F32))
    else:
        in_specs += [_const_spec((d_model, N_HEADS)), _const_spec((1, N_HEADS)), _const_spec((1, A))]
        args += [p["w_f"], p["b_f"], p["kg"]]
        out_shape = [tok(A, q_dtype), tok(A, F32), tok(A, F32), tok(N_HEADS, F32)] + tail_shapes
        out_specs = [row(A), row(A), row(A), row(N_HEADS)] + tail_specs

    return pl.pallas_call(
        functools.partial(_in_proj_body, tiles_per_seq=tiles_per_seq, feature_major=feature_major,
                          sub_rows=sub_rows),
        grid=(n_tiles,),
        in_specs=in_specs,
        out_specs=out_specs,
        out_shape=out_shape,
        scratch_shapes=scratch,
        compiler_params=pltpu.CompilerParams(dimension_semantics=("arbitrary",),
                                             vmem_limit_bytes=VMEM_LIMIT_BYTES),
        name="in_proj_fm" if feature_major else "in_proj",
    )(*args)


def _attn_prompt_body(q_ref, kt_ref, vt_ref, c_ref, o_ref, kb_ref, vb_ref, *, tq):
    hp = pl.program_id(1)
    seq = q_ref.shape[0]
    vb_ref[...] = vt_ref[...].astype(BF16)
    lane = lax.broadcasted_iota(jnp.int32, (1, LANES), 1)
    first_head = lane < HEAD_DIM
    causal = (lax.broadcasted_iota(jnp.int32, (tq, tq), 1)
              <= lax.broadcasted_iota(jnp.int32, (tq, tq), 0))
    kt = kt_ref[...]
    c_all = c_ref[...] * LOG2E
    sub = lax.broadcasted_iota(jnp.int32, c_all.shape, 0)
    q_fill = []
    for hh in range(HEADS_PER_VREG):
        c_row = jnp.sum(jnp.where(sub == HEADS_PER_VREG * hp + hh, c_all, 0.0), axis=0, keepdims=True)
        hi, mid, lo = (part.astype(F32) for part in _split3(-c_row))
        base = (1 - hh) * HEAD_DIM
        blk = kt[base:base + SUBLANES, :]
        blk = jnp.where(sub == 0, hi, jnp.where(sub == 1, mid, jnp.where(sub == 2, lo, blk)))
        pieces = [kt[0:base, :]] * (base > 0) + [blk, kt[base + SUBLANES:, :]]
        kb_ref[hh] = jnp.concatenate(pieces, axis=0).astype(BF16)
        q_fill.append(jnp.where((lane >= base) & (lane < base + 3), 1.0, 0.0).astype(BF16))

    for i in range(seq // tq):
        q0 = i * tq
        q = q_ref[q0:q0 + tq, :]
        outs = []
        for hh in range(HEADS_PER_VREG):
            own = first_head if hh == 0 else jnp.logical_not(first_head)
            qm = jnp.where(own, q, q_fill[hh])
            m = jnp.full((tq, 1), NEG_BIG, F32)
            l = jnp.zeros((tq, 1), F32)
            acc = jnp.zeros((tq, LANES), F32)
            for j in range(i + 1):
                k0 = j * tq
                s = _dot(qm, kb_ref[hh, :, k0:k0 + tq])
                if j == i:
                    s = jnp.where(causal, s, NEG_BIG)
                m_new = jnp.maximum(m, jnp.max(s, axis=1, keepdims=True))
                alpha = jnp.exp2(m - m_new)
                pr = jnp.exp2(s - m_new)
                l = alpha * l + jnp.sum(pr, axis=1, keepdims=True)
                acc = alpha * acc + _dot_nt(pr.astype(BF16), vb_ref[:, k0:k0 + tq])
                m = m_new
            outs.append(acc * (1.0 / l))
        o_ref[q0:q0 + tq, :] = jnp.where(first_head, outs[0], outs[1]).astype(o_ref.dtype)


def _attn_prompt(q2d, kt, vt, ct, *, tq):
    batch, _, seq_len = kt.shape
    n = q2d.shape[0]
    pairs = N_HEADS // HEADS_PER_VREG
    fm = pl.BlockSpec((None, LANES, seq_len), lambda b, hp: (b, hp, 0))
    tok = pl.BlockSpec((seq_len, LANES), lambda b, hp: (b, hp))
    return pl.pallas_call(
        functools.partial(_attn_prompt_body, tq=tq),
        grid=(batch, pairs),
        in_specs=[tok, fm, fm, pl.BlockSpec((N_HEADS, seq_len), lambda b, hp: (b, 0))],
        out_specs=tok,
        out_shape=jax.ShapeDtypeStruct((n, ATTN_WIDTH), BF16),
        scratch_shapes=[pltpu.VMEM((HEADS_PER_VREG, LANES, seq_len), BF16), pltpu.VMEM((LANES, seq_len), BF16)],
        compiler_params=pltpu.CompilerParams(dimension_semantics=("arbitrary", "arbitrary"),
                                             vmem_limit_bytes=VMEM_LIMIT_BYTES),
        name="attn_prompt",
    )(q2d, kt, vt, ct)


def _attn_sample_body(pt_ref, q_ref, ks_ref, vs_ref, lfs_ref, *rest, n_pg, dec_seq, conv_steps_per_seq):
    kp = rest[0:n_pg]
    vp = rest[n_pg:2 * n_pg]
    lp = rest[2 * n_pg:3 * n_pg]
    lincl_ref, sel_ref, uincl_ref = rest[3 * n_pg:3 * n_pg + 3]
    rest = rest[3 * n_pg + 3:]
    if conv_steps_per_seq:
        (u_ref, cw_ref, cb_ref, lg_ref, lb_ref, o_ref, ac_ref,
         qbd_ref, m_ref, l_ref, acc_ref, tot_ref, lfpad_ref, new_ref, ext_ref) = rest
    else:
        o_ref, qbd_ref, m_ref, l_ref, acc_ref, tot_ref, lfpad_ref, new_ref = rest
    del pt_ref

    g = pl.program_id(1)
    if conv_steps_per_seq:
        @pl.when((pl.program_id(0) * pl.num_programs(1) + g) % conv_steps_per_seq == 0)
        def _():
            _conv_reset(ext_ref, u_ref.shape[0])

    rows = N_HEADS * dec_seq
    page = lfpad_ref.shape[0]
    row_head = lax.broadcasted_iota(jnp.int32, (rows, ATTN_WIDTH), 0) // dec_seq
    col_head = lax.broadcasted_iota(jnp.int32, (rows, ATTN_WIDTH), 1) // HEAD_DIM
    own_head = row_head == col_head

    def softmax_update(s, pv_fn):
        m = m_ref[...]
        m_new = jnp.maximum(m, jnp.max(s, axis=1, keepdims=True))
        alpha = jnp.exp(m - m_new)
        pr = jnp.exp(s - m_new)
        l_ref[...] = alpha * l_ref[...] + jnp.sum(pr, axis=1, keepdims=True)
        acc_ref[...] = alpha * acc_ref[...] + pv_fn(pr.astype(BF16))
        m_ref[...] = m_new

    @pl.when(g == 0)
    def _():
        q = q_ref[...]
        q_rows = jnp.concatenate([q] * N_HEADS, axis=0)
        qbd_ref[...] = jnp.where(own_head, q_rows, 0.0).astype(BF16)
        m_ref[...] = jnp.full(m_ref.shape, NEG_BIG, F32)
        l_ref[...] = jnp.zeros(l_ref.shape, F32)
        acc_ref[...] = jnp.zeros(acc_ref.shape, F32)
        tot_ref[...] = jnp.zeros(tot_ref.shape, F32)
        key = lax.broadcasted_iota(jnp.int32, (rows, page), 1)
        t = lax.broadcasted_iota(jnp.int32, (rows, page), 0) % dec_seq
        for slot, src in ((0, ks_ref), (1, vs_ref)):
            new_ref[slot] = jnp.zeros(new_ref.shape[1:], F32)
            new_ref[slot, 0:dec_seq, :] = src[...]
        k_new = new_ref[0].astype(BF16)
        v_new = new_ref[1].astype(BF16)
        lfpad_ref[...] = jnp.zeros(lfpad_ref.shape, F32)
        lfpad_ref[0:dec_seq, 0:N_HEADS] = lfs_ref[...]
        lincl = lincl_ref[...]
        pre = jnp.zeros((page, LANES), F32)
        for part in _split3(lfpad_ref[...]):
            pre = pre + _dot(lincl, part)
        sel = sel_ref[...]
        pre_rows = jnp.zeros((rows, page), F32)
        for part in _split3(pre):
            pre_rows = pre_rows + _dot_nt(sel, part)
        tot_new = pre_rows[:, page - 1:page]
        s = _dot_nt(qbd_ref[...], k_new) + (tot_new - pre_rows)
        softmax_update(jnp.where(key <= t, s, NEG_BIG), lambda pr: _dot(pr, v_new))
        tot_ref[...] = tot_new

    if conv_steps_per_seq:
        _conv_tile(u_ref, cw_ref, cb_ref, lg_ref, lb_ref, ac_ref, ext_ref, rows_per_chunk=u_ref.shape[0])

    lft = jnp.concatenate([lp[i][...] for i in range(n_pg)], axis=0)
    uincl = uincl_ref[...]
    pre_all = jnp.zeros(lft.shape, F32)
    for part in _split3(lft):
        pre_all = pre_all + _dot(part, uincl)
    qbd = qbd_ref[...]
    run = tot_ref[...]
    scores = [None] * n_pg
    for i in reversed(range(n_pg)):
        pre_rows = jnp.concatenate(
            [jnp.broadcast_to(pre_all[i * N_HEADS + hd:i * N_HEADS + hd + 1, :], (dec_seq, page))
             for hd in range(N_HEADS)], axis=0)
        run = run + pre_rows[:, page - 1:page]
        scores[i] = _dot(qbd, kp[i][...].astype(BF16)) + (run - pre_rows)
    tot_ref[...] = run

    def pv_pages(pr):
        out = _dot_nt(pr[:, 0:page], vp[0][...].astype(BF16))
        for i in range(1, n_pg):
            out = out + _dot_nt(pr[:, i * page:(i + 1) * page], vp[i][...].astype(BF16))
        return out

    softmax_update(jnp.concatenate(scores, axis=1), pv_pages)

    @pl.when(g == pl.num_programs(1) - 1)
    def _():
        o = jnp.where(own_head, acc_ref[...] * (1.0 / l_ref[...]), 0.0)
        out = o[0:dec_seq, :]
        for hd in range(1, N_HEADS):
            out = out + o[hd * dec_seq:(hd + 1) * dec_seq, :]
        o_ref[...] = out


def _conv_rows_per_step(n_tokens, seq_len, total_steps):
    if n_tokens % total_steps:
        return 0
    r = n_tokens // total_steps
    ok = r % (2 * SUBLANES) == 0 and seq_len % r == 0 and HIST_ROWS <= r <= 512
    return r if ok else 0


def _attn_sample(q3, k3, v3, lf3, cache_kt, cache_vt, cache_lft, page_table, *, n_pg, conv=None):
    dec_batch, dec_seq, _ = q3.shape
    n_pool, _, page = cache_kt.shape
    n_pages = page_table.shape[1]
    assert n_pages % n_pg == 0 and dec_seq == SUBLANES and page == LANES
    steps = n_pages // n_pg
    rows = N_HEADS * dec_seq

    uincl = _upper_incl(page)
    sel = (lax.broadcasted_iota(jnp.int32, (rows, LANES), 0) // dec_seq
           == lax.broadcasted_iota(jnp.int32, (rows, LANES), 1)).astype(BF16)

    def page_map(slot):
        def index_map(b, g, pt):
            return (pt[b * n_pages + n_pages - (g + 1) * n_pg + slot], 0, 0)
        return index_map

    per_b = lambda c: pl.BlockSpec((None, dec_seq, c), lambda b, g, pt: (b, 0, 0))
    const = lambda shape: pl.BlockSpec(shape, lambda b, g, pt: (0, 0))
    in_specs = [per_b(ATTN_WIDTH), per_b(ATTN_WIDTH), per_b(ATTN_WIDTH), per_b(N_HEADS)]
    in_specs += [pl.BlockSpec((None, ATTN_WIDTH, page), page_map(i)) for i in range(n_pg)]
    in_specs += [pl.BlockSpec((None, ATTN_WIDTH, page), page_map(i)) for i in range(n_pg)]
    in_specs += [pl.BlockSpec((None, N_HEADS, page), page_map(i)) for i in range(n_pg)]
    in_specs += [const((page, page)), const((rows, LANES)), const((page, page))]
    out_specs = [pl.BlockSpec((None, dec_seq, ATTN_WIDTH), lambda b, g, pt: (b, 0, 0))]
    out_shape = [jax.ShapeDtypeStruct((dec_batch, dec_seq, ATTN_WIDTH), F32)]
    scratch = [pltpu.VMEM((rows, ATTN_WIDTH), BF16), pltpu.VMEM((rows, 1), F32),
               pltpu.VMEM((rows, 1), F32), pltpu.VMEM((rows, ATTN_WIDTH), F32),
               pltpu.VMEM((rows, 1), F32), pltpu.VMEM((page, LANES), F32),
               pltpu.VMEM((2, page, ATTN_WIDTH), F32)]
    args = [page_table.reshape(-1), q3, k3, v3, lf3]
    args += [cache_kt] * n_pg + [cache_vt] * n_pg + [cache_lft] * n_pg + [uincl.T, sel, uincl]
    conv_steps_per_seq = 0
    if conv is not None:
        u2d, p, seq_len, r = conv
        ch = u2d.shape[1]
        conv_steps_per_seq = seq_len // r
        slice_spec = pl.BlockSpec((r, ch), lambda b, g, pt: (b * steps + g, 0))
        in_specs += [slice_spec, const((CONV_WIDTH, ch)), const((1, ch)), const((1, ch)), const((1, ch))]
        args += [u2d, p["conv_w"], p["conv_b"], p["ln_g"], p["ln_b"]]
        out_specs.append(slice_spec)
        out_shape.append(jax.ShapeDtypeStruct(u2d.shape, BF16))
        scratch.append(pltpu.VMEM((HIST_ROWS + r + SUBLANES, ch), F32))
    grid_spec = pltpu.PrefetchScalarGridSpec(
        num_scalar_prefetch=1,
        grid=(dec_batch, steps),
        in_specs=in_specs,
        out_specs=out_specs,
        scratch_shapes=scratch,
    )
    outs = pl.pallas_call(
        functools.partial(_attn_sample_body, n_pg=n_pg, dec_seq=dec_seq, conv_steps_per_seq=conv_steps_per_seq),
        grid_spec=grid_spec,
        out_shape=out_shape,
        compiler_params=pltpu.CompilerParams(dimension_semantics=("arbitrary", "arbitrary"),
                                             vmem_limit_bytes=VMEM_LIMIT_BYTES),
        name="attn_sample_conv" if conv is not None else "attn_sample",
    )(*args)
    return outs if conv is not None else outs[0]


def _ln_swish(c, g, b):
    mu = jnp.mean(c, axis=-1, keepdims=True)
    d = c - mu
    var = jnp.mean(d * d, axis=-1, keepdims=True)
    y = d * lax.rsqrt(var + EPS) * g + b
    return y * jax.nn.sigmoid(y)


HIST_ROWS = 32
HIST_PAD = HIST_ROWS - (CONV_WIDTH - 1)
TAP_GROUPS = -(-(HIST_PAD + CONV_WIDTH) // SUBLANES)


def _conv_reset(ext_ref, tc):
    ch = ext_ref.shape[1]
    ext_ref[0:HIST_ROWS, :] = jnp.zeros((HIST_ROWS, ch), F32)
    ext_ref[HIST_ROWS + tc:HIST_ROWS + tc + SUBLANES, :] = jnp.zeros((SUBLANES, ch), F32)


def _conv_tile(u_ref, w_ref, b_ref, g_ref, beta_ref, o_ref, ext_ref, *, rows_per_chunk):
    tc, ch = u_ref.shape
    R = rows_per_chunk
    ext_ref[HIST_ROWS:HIST_ROWS + tc, :] = u_ref[...]

    for r0 in range(0, tc, R):
        cols = []
        for cb in range(ch // LANES):
            lanes = slice(cb * LANES, (cb + 1) * LANES)
            y = jnp.zeros((R, LANES), F32) + b_ref[:, lanes]
            for r in range(SUBLANES):
                part = None
                for a in range(TAP_GROUPS):
                    k = SUBLANES * a + r - HIST_PAD
                    if 0 <= k < CONV_WIDTH:
                        term = ext_ref[r0 + SUBLANES * a:r0 + SUBLANES * a + R + SUBLANES, lanes] * w_ref[k:k + 1, lanes]
                        part = term if part is None else part + term
                y = y + part[r:r + R, :]
            cols.append(y)
        acc = jnp.concatenate(cols, axis=1)
        o_ref[r0:r0 + R, :] = _ln_swish(acc, g_ref[...], beta_ref[...]).astype(o_ref.dtype)
    ext_ref[0:HIST_ROWS, :] = ext_ref[tc:tc + HIST_ROWS, :]


def _conv_prompt_body(u_ref, w_ref, b_ref, g_ref, beta_ref, o_ref, ext_ref, *, rows_per_chunk):
    @pl.when(pl.program_id(1) == 0)
    def _():
        _conv_reset(ext_ref, u_ref.shape[0])

    _conv_tile(u_ref, w_ref, b_ref, g_ref, beta_ref, o_ref, ext_ref, rows_per_chunk=rows_per_chunk)


def _conv_prompt(u2d, p, *, batch, seq_len, tc):
    n, ch = u2d.shape
    tiles = seq_len // tc
    return pl.pallas_call(
        functools.partial(_conv_prompt_body, rows_per_chunk=_pick_tile(tc, 128)),
        grid=(batch, tiles),
        in_specs=[pl.BlockSpec((tc, ch), lambda b, i: (b * tiles + i, 0)),
                  _const_spec((CONV_WIDTH, ch)), _const_spec((1, ch)), _const_spec((1, ch)),
                  _const_spec((1, ch))],
        out_specs=pl.BlockSpec((tc, ch), lambda b, i: (b * tiles + i, 0)),
        out_shape=jax.ShapeDtypeStruct((n, ch), BF16),
        scratch_shapes=[pltpu.VMEM((HIST_ROWS + tc + SUBLANES, ch), F32)],
        compiler_params=pltpu.CompilerParams(dimension_semantics=("arbitrary", "arbitrary"),
                                             vmem_limit_bytes=VMEM_LIMIT_BYTES),
        name="conv_prompt",
    )(u2d, p["conv_w"], p["conv_b"], p["ln_g"], p["ln_b"])


def _conv_sample_body(st_ref, u_ref, w_ref, b_ref, g_ref, beta_ref, o_ref, ns_ref, ext_ref):
    hist = st_ref.shape[0]
    dec_seq = u_ref.shape[0]
    ext_ref[0:hist] = st_ref[...]
    ext_ref[hist:hist + dec_seq] = u_ref[...]
    for t in range(dec_seq):
        acc = jnp.zeros(ext_ref.shape[1:], F32) + b_ref[...]
        for k in range(CONV_WIDTH):
            acc = acc + ext_ref[t + k] * w_ref[k:k + 1, :]
        o_ref[t] = _ln_swish(acc, g_ref[...], beta_ref[...]).astype(o_ref.dtype)
    ns_ref[...] = ext_ref[dec_seq:dec_seq + hist]


def _conv_sample(state_t, u_t, p, *, seqs_per_step):
    dec_seq, dec_batch, ch = u_t.shape
    hist = CONV_WIDTH - 1
    blk = lambda r: pl.BlockSpec((r, seqs_per_step, ch), lambda i: (0, i, 0))
    return pl.pallas_call(
        _conv_sample_body,
        grid=(dec_batch // seqs_per_step,),
        in_specs=[blk(hist), blk(dec_seq), _const_spec((CONV_WIDTH, ch)), _const_spec((1, ch)),
                  _const_spec((1, ch)), _const_spec((1, ch))],
        out_specs=[blk(dec_seq), blk(hist)],
        out_shape=[jax.ShapeDtypeStruct((dec_seq, dec_batch, ch), F32),
                   jax.ShapeDtypeStruct((hist, dec_batch, ch), F32)],
        scratch_shapes=[pltpu.VMEM((hist + dec_seq, seqs_per_step, ch), F32)],
        compiler_params=pltpu.CompilerParams(dimension_semantics=("arbitrary",),
                                             vmem_limit_bytes=VMEM_LIMIT_BYTES),
        name="conv_sample",
    )(state_t, u_t, p["conv_w"], p["conv_b"], p["ln_g"], p["ln_b"])


def _merge_mlp_body(x_ref, o_ref, ac_ref, ga_ref, gb_ref, wao_ref, wco_ref, wout_ref, g2_ref,
                    wup_ref, wdn_ref, y_ref, *, ff_chunk):
    ya = _dot(o_ref[...].astype(BF16), wao_ref[...])
    yc = _dot(ac_ref[...].astype(BF16), wco_ref[...])
    mix = ga_ref[...].astype(F32) * ya + gb_ref[...].astype(F32) * yc
    x1 = x_ref[...] + _dot(mix.astype(BF16), wout_ref[...])
    ms = jnp.mean(x1 * x1, axis=-1, keepdims=True)
    h2 = (x1 * lax.rsqrt(ms + EPS) * g2_ref[...]).astype(BF16)
    d_ff = wup_ref.shape[1]
    acc = x1
    for c in range(d_ff // ff_chunk):
        a = jnp.maximum(_dot(h2, wup_ref[:, c * ff_chunk:(c + 1) * ff_chunk]), 0.0)
        acc = acc + _dot((a * a).astype(BF16), wdn_ref[c * ff_chunk:(c + 1) * ff_chunk, :])
    y_ref[...] = acc


def _merge_mlp(x2d, o2d, ac2d, ga, gb, p, *, tm):
    n, d_model = x2d.shape
    d_ff = p["w_up"].shape[1]
    ch = ac2d.shape[1]
    row = lambda c: pl.BlockSpec((tm, c), lambda i: (i, 0))
    return pl.pallas_call(
        functools.partial(_merge_mlp_body, ff_chunk=512),
        grid=(n // tm,),
        in_specs=[row(d_model), row(ATTN_WIDTH), row(ch), row(d_model), row(d_model),
                  _const_spec((ATTN_WIDTH, d_model)), _const_spec((ch, d_model)),
                  _const_spec((d_model, d_model)), _const_spec((1, d_model)),
                  _const_spec((d_model, d_ff)), _const_spec((d_ff, d_model))],
        out_specs=row(d_model),
        out_shape=jax.ShapeDtypeStruct((n, d_model), F32),
        compiler_params=pltpu.CompilerParams(dimension_semantics=("arbitrary",),
                                             vmem_limit_bytes=VMEM_LIMIT_BYTES),
        name="merge_mlp",
    )(x2d, o2d, ac2d, ga, gb, p["w_attn_out"], p["w_conv_out"], p["w_out"], p["g2"], p["w_up"], p["w_down"])


def _prep_layer_params(norm1_g, w_in, b_forget, b_gate, q_norm_g, k_norm_g, w_attn_out, conv_dw_w,
                       conv_dw_b, conv_ln_g, conv_ln_b, w_conv_out, w_out, norm2_g, w_up, w_down):
    A = ATTN_WIDTH
    d_model = w_in.shape[0]
    ch = conv_dw_w.shape[1]
    f0 = 3 * A
    a0 = f0 + N_HEADS
    w_bf = w_in.astype(BF16)
    w_f = w_bf[:, f0:a0]
    w_cat = jnp.concatenate([w_bf[:, 0:f0], w_bf[:, a0:]], axis=1)
    hm = (lax.broadcasted_iota(jnp.int32, (A, A), 0) // HEAD_DIM
          == lax.broadcasted_iota(jnp.int32, (A, A), 1) // HEAD_DIM).astype(BF16) * (1.0 / HEAD_DIM)
    kg = jnp.tile(k_norm_g, N_HEADS)
    return dict(
        g1=norm1_g.reshape(1, d_model), w_cat=w_cat, w_f=w_f, w_f_t=w_f.T, w_kv_t=w_bf[:, A:f0].T,
        b_f=b_forget.reshape(1, N_HEADS), b_f_t=b_forget.reshape(N_HEADS, 1),
        b_gate=b_gate.reshape(1, 2 * d_model),
        qg=jnp.tile(q_norm_g, N_HEADS).reshape(1, A) * (HEAD_DIM ** -0.5),
        qg_log2=jnp.tile(q_norm_g, N_HEADS).reshape(1, A) * (HEAD_DIM ** -0.5 * LOG2E),
        kg=kg.reshape(1, A), kg_t=kg.reshape(A, 1),
        head_mean=hm.astype(BF16),
        conv_w=conv_dw_w, conv_b=conv_dw_b.reshape(1, ch), ln_g=conv_ln_g.reshape(1, ch),
        ln_b=conv_ln_b.reshape(1, ch),
        w_attn_out=w_attn_out.astype(BF16), w_conv_out=w_conv_out.astype(BF16),
        w_out=w_out.astype(BF16), g2=norm2_g.reshape(1, d_model),
        w_up=w_up.astype(BF16), w_down=w_down.astype(BF16))


def _pick_tile(n, pref):
    t = min(n, pref)
    while n % t:
        t //= 2
    return t


def _layer(xp, xs, cache_k, cache_v, cache_lf, state_conv, page_table, p):
    batch, seq_len, d_model = xp.shape
    dec_batch, dec_seq, _ = xs.shape
    n_pool, page, _, _ = cache_k.shape
    hist = CONV_WIDTH - 1

    n_p = batch * seq_len
    tm = _pick_tile(seq_len, 512)
    q, kt, vt, lft, u, ga, gb, ct = _in_proj(xp.reshape(n_p, d_model), p, tm=_pick_tile(seq_len, 1024),
                                             seq_len=seq_len, feature_major=True, q_dtype=BF16)
    o = _attn_prompt(q, kt, vt, ct, tq=_pick_tile(seq_len, 512))
    to_tok = lambda a: jnp.transpose(a.reshape(batch, N_HEADS, HEAD_DIM, seq_len), (0, 3, 1, 2))
    kp, vp = to_tok(kt), to_tok(vt)
    lp = jnp.transpose(lft, (0, 2, 1))
    cp = u.reshape(batch, seq_len, -1)[:, seq_len - hist:, :]

    n_s = dec_batch * dec_seq
    tms = _pick_tile(n_s, 512)
    qs, ks, vs, lfs, us, gas, gbs = _in_proj(xs.reshape(n_s, d_model), p, tm=tms, seq_len=dec_seq,
                                             feature_major=False, q_dtype=F32)
    r3 = lambda a: a.reshape(dec_batch, dec_seq, a.shape[-1])
    fm_cache = lambda c: jnp.transpose(c, (0, 2, 3, 1)).reshape(n_pool, ATTN_WIDTH, page)
    n_pg = _pick_tile(page_table.shape[1], 32)
    sample_args = (r3(qs), r3(ks), r3(vs), r3(lfs), fm_cache(cache_k), fm_cache(cache_v),
                   jnp.transpose(cache_lf, (0, 2, 1)), page_table)
    conv_rows = _conv_rows_per_step(n_p, seq_len, dec_batch * (page_table.shape[1] // n_pg))
    if conv_rows:
        os_, ac = _attn_sample(*sample_args, n_pg=n_pg, conv=(u, p, seq_len, conv_rows))
    else:
        os_ = _attn_sample(*sample_args, n_pg=n_pg)
        ac = _conv_prompt(u, p, batch=batch, seq_len=seq_len, tc=tm)
    yp = _merge_mlp(xp.reshape(n_p, d_model), o, ac, ga, gb, p, tm=tm).reshape(batch, seq_len, d_model)
    time_major = lambda a: jnp.transpose(a, (1, 0, 2))
    acs_t, new_state_t = _conv_sample(time_major(state_conv), time_major(r3(us)), p,
                                      seqs_per_step=_pick_tile(dec_batch, 32))
    acs, new_state = time_major(acs_t), time_major(new_state_t)
    ys = _merge_mlp(xs.reshape(n_s, d_model), os_.reshape(n_s, ATTN_WIDTH), acs.reshape(n_s, -1),
                    gas, gbs, p, tm=tms).reshape(dec_batch, dec_seq, d_model)
    k_s = ks.reshape(dec_batch, dec_seq, N_HEADS, HEAD_DIM)
    v_s = vs.reshape(dec_batch, dec_seq, N_HEADS, HEAD_DIM)
    l_s = lfs.reshape(dec_batch, dec_seq, N_HEADS)
    return yp, ys, kp, vp, lp, cp, k_s, v_s, l_s, new_state


def kernel(x_prompt, x_sample, cache_k, cache_v, cache_logf, state_conv, page_table, norm1_g, w_in, b_forget, b_gate, q_norm_g, k_norm_g, w_attn_out, conv_dw_w, conv_dw_b, conv_ln_g, conv_ln_b, w_conv_out, w_out, norm2_g, w_up, w_down):
    depth = w_in.shape[0]
    xp, xs = x_prompt, x_sample
    outs = [[] for _ in range(8)]
    for l in range(depth):
        p = _prep_layer_params(norm1_g[l], w_in[l], b_forget[l], b_gate[l], q_norm_g[l], k_norm_g[l],
                               w_attn_out[l], conv_dw_w[l], conv_dw_b[l], conv_ln_g[l], conv_ln_b[l],
                               w_conv_out[l], w_out[l], norm2_g[l], w_up[l], w_down[l])
        xp, xs, *rest = _layer(xp, xs, cache_k[l], cache_v[l], cache_logf[l], state_conv[l], page_table, p)
        for acc, r in zip(outs, rest):
            acc.append(r)
    return (xp, xs) + tuple(jnp.stack(o) for o in outs)
```
